```python
import math
import jax, jax.numpy as jnp
from jax import lax
import numpy as np

D_MODEL = 1024
BATCH = 8
SEQ = 4096
DEPTH = 2

CTX_LEN = 256
GRID_W = 64
N_EVEN = (DEPTH + 1) // 2
N_ODD = DEPTH // 2
EPS = 1e-6
NEG_INF = -1e30

ATT_HEADS = 8
ATT_KV_HEADS = 2
ATT_GROUP = ATT_HEADS // ATT_KV_HEADS
ATT_HEAD_DIM = 64
WINDOW = 128
ATT_BLOCK = 128
ROPE_BASE = 10000.0

DN_HEADS = 8
DN_HEAD_DIM = 64
DN_WIDTH = DN_HEADS * DN_HEAD_DIM
DN_CONV = 5
DN_CHUNK = 64

HYB_SPLITS = (ATT_HEADS * ATT_HEAD_DIM, ATT_KV_HEADS * ATT_HEAD_DIM, ATT_KV_HEADS * ATT_HEAD_DIM,
              3 * DN_WIDTH, DN_WIDTH, 2 * DN_HEADS, 2 * DN_HEADS)
HYB_IN = sum(HYB_SPLITS)
HYB_MIX = ATT_HEADS * ATT_HEAD_DIM + DN_WIDTH

SSM_D_INNER = 2 * D_MODEL
SSM_HEAD_DIM = 64
SSM_HEADS = SSM_D_INNER // SSM_HEAD_DIM
SSM_GROUPS = 4
SSM_STATE = 128
SSM_CONV = 5
SSM_CHUNK = 128
SSM_CONV_DIM = SSM_D_INNER + 2 * SSM_GROUPS * SSM_STATE
SSM_SPLITS = (SSM_D_INNER, SSM_CONV_DIM, 2 * SSM_HEADS)
SSM_IN = sum(SSM_SPLITS)

FFN_DIM = 2816
FFN_CONV = 3

kernel_name = 'hybrid_swa_deltanet_ssd_prefix_dit'

f32 = jnp.float32


def _split(t, sizes):
    return jnp.split(t, np.cumsum(sizes)[:-1].tolist(), axis=-1)


def rms_norm(x, g):
    xf = x.astype(f32)
    y = xf * lax.rsqrt(jnp.mean(xf * xf, axis=-1, keepdims=True) + EPS)
    return (y * g.astype(f32)).astype(x.dtype)


def l2norm(t):
    return t * lax.rsqrt(jnp.sum(t * t, axis=-1, keepdims=True) + EPS)


def modulate(h, shift, scale):
    return h * (1 + scale) + shift


def modulation(cond, w, b):
    return jnp.split(jax.nn.silu(cond) @ w + b, 6, axis=-1)


def dwconv(x, w):
    pad = w.shape[0] // 2
    return lax.conv_general_dilated(x, w[:, None, :].astype(x.dtype), window_strides=(1,),
                                    padding=[(pad, pad)], dimension_numbers=('NWC', 'WIO', 'NWC'),
                                    feature_group_count=x.shape[-1])


def axial_rope(rows):
    row = jnp.repeat(jnp.arange(rows, dtype=f32), GRID_W)
    col = jnp.tile(jnp.arange(GRID_W, dtype=f32), rows)
    n_freq = ATT_HEAD_DIM // 4
    inv = ROPE_BASE ** (-jnp.arange(n_freq, dtype=f32) / n_freq)
    ang = jnp.concatenate([row[:, None] * inv, col[:, None] * inv], axis=-1)
    return jnp.cos(ang), jnp.sin(ang)


def apply_rope(t, cos, sin):
    half = t.shape[-1] // 2
    tf = t.astype(f32)
    t1, t2 = tf[..., :half], tf[..., half:]
    cs, sn = cos[None, :, None, :], sin[None, :, None, :]
    return jnp.concatenate([t1 * cs - t2 * sn, t1 * sn + t2 * cs], axis=-1).astype(t.dtype)


def sink_softmax(scores, sink):
    m = sink
    for s in scores:
        m = jnp.maximum(m, jnp.max(s, axis=-1, keepdims=True))
    ps = [jnp.exp(s - m) for s in scores]
    denom = jnp.exp(sink - m) + sum(jnp.sum(p, axis=-1, keepdims=True) for p in ps)
    return [p / denom for p in ps]


def window_attention_latent(q, k, v, k_ctx, v_ctx, sink):
    Bsz, N = q.shape[:2]
    nb = N // ATT_BLOCK
    scale = ATT_HEAD_DIM ** -0.5
    sink_b = sink.astype(f32).reshape(ATT_KV_HEADS, ATT_GROUP, 1, 1)
    qb = q.reshape(Bsz, nb, ATT_BLOCK, ATT_KV_HEADS, ATT_GROUP, ATT_HEAD_DIM)

    def band(t):
        tp = jnp.pad(t, ((0, 0), (ATT_BLOCK, ATT_BLOCK), (0, 0), (0, 0)))
        tp = tp.reshape(Bsz, nb + 2, ATT_BLOCK, ATT_KV_HEADS, ATT_HEAD_DIM)
        return jnp.concatenate([tp[:, :-2], tp[:, 1:-1], tp[:, 2:]], axis=2)

    kw, vw = band(k), band(v)
    qi = jnp.arange(ATT_BLOCK)[:, None]
    kj = jnp.arange(3 * ATT_BLOCK)[None, :]
    key_pos = jnp.arange(nb)[:, None, None] * ATT_BLOCK - ATT_BLOCK + kj
    mask = (jnp.abs(kj - ATT_BLOCK - qi) <= WINDOW) & (key_pos >= 0) & (key_pos < N)
    s_loc = jnp.einsum('bnqkgd,bnskd->bnkgqs', qb, kw, preferred_element_type=f32) * scale
    s_loc = jnp.where(mask[None, :, None, None], s_loc, NEG_INF)
    s_ctx = jnp.einsum('bnqkgd,bckd->bnkgqc', qb, k_ctx, preferred_element_type=f32) * scale
    p_loc, p_ctx = sink_softmax([s_loc, s_ctx], sink_b)
    o = (jnp.einsum('bnkgqs,bnskd->bnqkgd', p_loc, vw.astype(f32))
         + jnp.einsum('bnkgqc,bckd->bnqkgd', p_ctx, v_ctx.astype(f32)))
    return o.reshape(Bsz, N, ATT_HEADS * ATT_HEAD_DIM)


def context_attention(q, k, v, sink):
    Bsz, L = q.shape[:2]
    sink_b = sink.astype(f32).reshape(ATT_KV_HEADS, ATT_GROUP, 1, 1)
    s = jnp.einsum('bqkgd,bckd->bkgqc', q, k, preferred_element_type=f32) * ATT_HEAD_DIM ** -0.5
    p = sink_softmax([s], sink_b)[0]
    o = jnp.einsum('bkgqc,bckd->bqkgd', p, v.astype(f32))
    return o.reshape(Bsz, L, ATT_HEADS * ATT_HEAD_DIM)


def gated_delta_chunked(q, k, v, beta, logd, S0, with_out):
    Bsz, L, H, _ = q.shape
    dv = v.shape[-1]
    C = DN_CHUNK
    nc = L // C

    def to_chunks(t):
        return jnp.moveaxis(t.reshape(Bsz, nc, C, H, -1), 2, 3)

    qc, kc, vc = to_chunks(q), to_chunks(k), to_chunks(v)
    bc = jnp.moveaxis(beta.reshape(Bsz, nc, C, H), 2, 3)
    g = jnp.cumsum(jnp.moveaxis(logd.reshape(Bsz, nc, C, H), 2, 3), axis=-1)
    diff = g[..., :, None] - g[..., None, :]
    idx = jnp.arange(C)
    strict = idx[:, None] > idx[None, :]
    incl = idx[:, None] >= idx[None, :]
    lower = jnp.eye(C, dtype=q.dtype) + bc[..., :, None] * jnp.einsum(
        'bnhid,bnhjd->bnhij', kc, kc) * jnp.exp(jnp.where(strict, diff, -jnp.inf))
    u0 = lax.linalg.triangular_solve(lower, bc[..., None] * vc, left_side=True, lower=True,
                                     unit_diagonal=True)
    wk = lax.linalg.triangular_solve(lower, (bc * jnp.exp(g))[..., None] * kc, left_side=True,
                                     lower=True, unit_diagonal=True)
    kd = kc * jnp.exp(g[..., -1:] - g)[..., None]
    g_end = jnp.exp(g[..., -1])
    xs = [u0, wk, kd, g_end]
    if with_out:
        mq = jnp.einsum('bnhid,bnhjd->bnhij', qc, kc) * jnp.exp(jnp.where(incl, diff, -jnp.inf))
        xs += [mq, qc * jnp.exp(g)[..., None]]
    xs = tuple(jnp.moveaxis(t, 1, 0) for t in xs)

    def step(S, inp):
        u = inp[0] - jnp.einsum('bhcd,bhde->bhce', inp[1], S)
        S_new = inp[3][..., None, None] * S + jnp.einsum('bhcd,bhce->bhde', inp[2], u)
        if with_out:
            o = jnp.einsum('bhcd,bhde->bhce', inp[5], S) + jnp.einsum('bhij,bhje->bhie', inp[4], u)
            return S_new, o
        return S_new, None

    S_fin, o = lax.scan(step, S0, xs)
    if not with_out:
        return None, S_fin
    o = jnp.moveaxis(jnp.moveaxis(o, 0, 1), 3, 2).reshape(Bsz, L, H, dv)
    return o, S_fin


def bidir_delta(dn_c, dn_l, with_ctx_out):
    Bsz = dn_l[0].shape[0]
    out_c, out_l = [], []
    for d in range(2):
        fl = (lambda t: jnp.flip(t, axis=1)) if d else (lambda t: t)

        def args(dn):
            q, k, v, beta, logd = dn
            return fl(q), fl(k), fl(v), fl(beta[:, :, d]), fl(logd[:, :, d])

        S0 = jnp.zeros((Bsz, DN_HEADS, DN_HEAD_DIM, DN_HEAD_DIM), f32)
        oc, Sc = gated_delta_chunked(*args(dn_c), S0, with_ctx_out)
        ol, _ = gated_delta_chunked(*args(dn_l), Sc, True)
        out_l.append(fl(ol))
        if with_ctx_out:
            out_c.append(fl(oc))
    return (out_c[0] + out_c[1]) if with_ctx_out else None, out_l[0] + out_l[1]


def attn_delta_mixer(h_ctx, h_lat, w_in, w_out, sink, dn_conv, dn_a_log, dn_dt_bias, dn_norm_g,
                     cos, sin, with_ctx_out):
    def project(h):
        Bsz, L = h.shape[:2]
        aq, ak, av, dqkv, dz, dbeta, dalpha = _split(h @ w_in, HYB_SPLITS)
        dqkv = jax.nn.silu(dwconv(dqkv, dn_conv)).astype(f32)
        dq, dk, dv = [t.reshape(Bsz, L, DN_HEADS, DN_HEAD_DIM) for t in jnp.split(dqkv, 3, axis=-1)]
        beta = jax.nn.sigmoid(dbeta.astype(f32)).reshape(Bsz, L, 2, DN_HEADS)
        logd = -jnp.exp(dn_a_log.astype(f32)) * jax.nn.softplus(
            dalpha.astype(f32).reshape(Bsz, L, 2, DN_HEADS) + dn_dt_bias.astype(f32))
        dn = (l2norm(dq) * DN_HEAD_DIM ** -0.5, l2norm(dk), dv, beta, logd)
        return (aq.reshape(Bsz, L, ATT_HEADS, ATT_HEAD_DIM),
                ak.reshape(Bsz, L, ATT_KV_HEADS, ATT_HEAD_DIM),
                av.reshape(Bsz, L, ATT_KV_HEADS, ATT_HEAD_DIM), dn, dz)

    def grp(q):
        return q.reshape(q.shape[0], q.shape[1], ATT_KV_HEADS, ATT_GROUP, ATT_HEAD_DIM)

    def merge(att, dn, z):
        Bsz, L = z.shape[:2]
        gated = rms_norm(dn, dn_norm_g) * jax.nn.silu(
            z.astype(f32).reshape(Bsz, L, DN_HEADS, DN_HEAD_DIM))
        cat = jnp.concatenate([att.astype(z.dtype), gated.reshape(Bsz, L, DN_WIDTH).astype(z.dtype)],
                              axis=-1)
        return cat @ w_out

    aq_c, ak_c, av_c, dn_c, dz_c = project(h_ctx)
    aq_l, ak_l, av_l, dn_l, dz_l = project(h_lat)
    aq_l, ak_l = apply_rope(aq_l, cos, sin), apply_rope(ak_l, cos, sin)
    att_l = window_attention_latent(grp(aq_l), ak_l, av_l, ak_c, av_c, sink)
    dn_sum_c, dn_sum_l = bidir_delta(dn_c, dn_l, with_ctx_out)
    out_l = merge(att_l, dn_sum_l, dz_l)
    if not with_ctx_out:
        return None, out_l
    att_c = context_attention(grp(aq_c), ak_c, av_c, sink)
    return merge(att_c, dn_sum_c, dz_c), out_l


def ssd_chunked(x, dt, a, bm, cm, S0, with_y):
    Bsz, L, H, P = x.shape
    G, N = bm.shape[2:]
    hpg = H // G
    C = SSM_CHUNK
    nc = L // C
    xc = x.reshape(Bsz, nc, C, G, hpg, P)
    dtc = dt.reshape(Bsz, nc, C, G, hpg)
    bc = bm.reshape(Bsz, nc, C, G, N)
    cc = cm.reshape(Bsz, nc, C, G, N)
    g = jnp.cumsum(dtc * a.reshape(G, hpg), axis=2)
    g_end = g[:, :, -1]
    xdt = xc * dtc[..., None]
    d_state = jnp.einsum('bclgs,bclgh,bclghp->bcghsp', bc, jnp.exp(g_end[:, :, None] - g), xdt)

    def step(S, inp):
        dS, ge = inp
        return jnp.exp(ge)[..., None, None] * S + dS, S

    S_fin, S_in = lax.scan(step, S0, (jnp.moveaxis(d_state, 1, 0), jnp.moveaxis(g_end, 1, 0)))
    if not with_y:
        return None, S_fin
    S_in = jnp.moveaxis(S_in, 0, 1)
    y_inter = jnp.einsum('bclgs,bcghsp->bclghp', cc, S_in) * jnp.exp(g)[..., None]
    gt = jnp.moveaxis(g, 2, 4)
    idx = jnp.arange(C)
    incl = idx[:, None] >= idx[None, :]
    decay = jnp.exp(jnp.where(incl, gt[..., :, None] - gt[..., None, :], -jnp.inf))
    scores = jnp.einsum('bclgs,bcmgs->bcglm', cc, bc)[:, :, :, None] * decay
    y_intra = jnp.einsum('bcghlm,bcmghp->bclghp', scores, xdt)
    return (y_inter + y_intra).reshape(Bsz, L, H, P), S_fin


def mamba_mixer(h_ctx, h_lat, w_in, conv_w, a_log, dt_bias, d_skip, norm_g, w_out, with_ctx_out):
    def project(h):
        Bsz, L = h.shape[:2]
        z, xbc, dt = _split(h @ w_in, SSM_SPLITS)
        xbc = jax.nn.silu(dwconv(xbc, conv_w)).astype(f32)
        xs, bm, cm = _split(xbc, (SSM_D_INNER, SSM_GROUPS * SSM_STATE, SSM_GROUPS * SSM_STATE))
        dt = jax.nn.softplus(dt.astype(f32).reshape(Bsz, L, 2, SSM_HEADS) + dt_bias.astype(f32))
        return (z, xs.reshape(Bsz, L, SSM_HEADS, SSM_HEAD_DIM),
                bm.reshape(Bsz, L, SSM_GROUPS, SSM_STATE), cm.reshape(Bsz, L, SSM_GROUPS, SSM_STATE), dt)

    def finish(y, xs, z):
        Bsz, L = z.shape[:2]
        y = y + d_skip.astype(f32)[:, None] * xs
        y = y.reshape(Bsz, L, SSM_D_INNER) * jax.nn.silu(z.astype(f32))
        y = rms_norm(y.reshape(Bsz, L, SSM_GROUPS, -1), norm_g.reshape(SSM_GROUPS, -1))
        return y.reshape(Bsz, L, SSM_D_INNER).astype(z.dtype) @ w_out

    z_c, x_c, b_c, c_c, dt_c = project(h_ctx)
    z_l, x_l, b_l, c_l, dt_l = project(h_lat)
    A = -jnp.exp(a_log.astype(f32))
    Bsz = h_lat.shape[0]
    y_c, y_l = [], []
    for d in range(2):
        fl = (lambda t: jnp.flip(t, axis=1)) if d else (lambda t: t)
        S0 = jnp.zeros((Bsz, SSM_GROUPS, SSM_HEADS // SSM_GROUPS, SSM_STATE, SSM_HEAD_DIM), f32)
        yc, Sc = ssd_chunked(fl(x_c), fl(dt_c[:, :, d]), A[d], fl(b_c), fl(c_c), S0, with_ctx_out)
        yl, _ = ssd_chunked(fl(x_l), fl(dt_l[:, :, d]), A[d], fl(b_l), fl(c_l), Sc, True)
        y_l.append(fl(yl))
        if with_ctx_out:
            y_c.append(fl(yc))
    out_l = finish(y_l[0] + y_l[1], x_l, z_l)
    if not with_ctx_out:
        return None, out_l
    return finish(y_c[0] + y_c[1], x_c, z_c), out_l


def conv_ffn(h, w_up, w_conv, w_down):
    u = dwconv(h @ w_up, w_conv)
    val, gate = jnp.split(u, 2, axis=-1)
    return (jax.nn.silu(gate) * val) @ w_down


def setup_inputs(seed: int = 0) -> dict:
    key = jax.random.key(seed)
    ks = iter(jax.random.split(key, 40))
    D = D_MODEL

    def nrm(shape, s):
        return jax.random.normal(next(ks), shape, f32) * s

    def gain(shape):
        return 1.0 + nrm(shape, 0.02)

    def decay_log(shape):
        return jnp.log(jax.random.uniform(next(ks), shape, f32, 1.0, 16.0))

    def dt_bias(shape):
        dt = jnp.exp(jax.random.uniform(next(ks), shape, f32, math.log(1e-3), math.log(1e-1)))
        return dt + jnp.log(-jnp.expm1(-dt))

    return {
        'x': nrm((BATCH, SEQ, D), 1.0),
        'c': nrm((BATCH, D), 1.0),
        'ctx': nrm((BATCH, CTX_LEN, D), 1.0),
        'c_ctx': nrm((D,), 1.0),
        'mod_w': nrm((DEPTH, D, 6 * D), 0.5 * D ** -0.5),
        'mod_b': nrm((DEPTH, 6 * D), 0.02),
        'norm_mix_g': gain((DEPTH, D)),
        'norm_ffn_g': gain((DEPTH, D)),
        'ffn_up': nrm((DEPTH, D, 2 * FFN_DIM), D ** -0.5),
        'ffn_conv': nrm((DEPTH, FFN_CONV, 2 * FFN_DIM), FFN_CONV ** -0.5),
        'ffn_down': nrm((DEPTH, FFN_DIM, D), FFN_DIM ** -0.5),
        'hyb_w_in': nrm((N_EVEN, D, HYB_IN), D ** -0.5),
        'hyb_w_out': nrm((N_EVEN, HYB_MIX, D), HYB_MIX ** -0.5),
        'att_sink': nrm((N_EVEN, ATT_HEADS), 0.5),
        'dn_conv': nrm((N_EVEN, DN_CONV, 3 * DN_WIDTH), DN_CONV ** -0.5),
        'dn_a_log': decay_log((N_EVEN, 2, DN_HEADS)),
        'dn_dt_bias': dt_bias((N_EVEN, 2, DN_HEADS)),
        'dn_norm_g': gain((N_EVEN, DN_HEAD_DIM)),
        'ssm_w_in': nrm((N_ODD, D, SSM_IN), D ** -0.5),
        'ssm_conv': nrm((N_ODD, SSM_CONV, SSM_CONV_DIM), SSM_CONV ** -0.5),
        'ssm_a_log': decay_log((N_ODD, 2, SSM_HEADS)),
        'ssm_dt_bias': dt_bias((N_ODD, 2, SSM_HEADS)),
        'ssm_d': gain((N_ODD, SSM_HEADS)),
        'ssm_norm_g': gain((N_ODD, SSM_D_INNER)),
        'ssm_w_out': nrm((N_ODD, SSM_D_INNER, D), SSM_D_INNER ** -0.5),
        'final_g': gain((D,)),
    }


def reference(x, c, ctx, c_ctx, mod_w, mod_b, norm_mix_g, norm_ffn_g, ffn_up, ffn_conv, ffn_down,
              hyb_w_in, hyb_w_out, att_sink, dn_conv, dn_a_log, dn_dt_bias, dn_norm_g,
              ssm_w_in, ssm_conv, ssm_a_log, ssm_dt_bias, ssm_d, ssm_norm_g, ssm_w_out, final_g):
    rows = x.shape[1] // GRID_W
    cos, sin = axial_rope(rows)
    y_ctx = ctx
    for i in range(DEPTH):
        last = i == DEPTH - 1
        sh1, sc1, gt1, sh2, sc2, gt2 = [m[:, None, :] for m in modulation(c, mod_w[i], mod_b[i])]
        csh1, csc1, cgt1, csh2, csc2, cgt2 = modulation(c_ctx, mod_w[i], mod_b[i])
        h_lat = modulate(rms_norm(x, norm_mix_g[i]), sh1, sc1)
        h_ctx = modulate(rms_norm(y_ctx, norm_mix_g[i]), csh1, csc1)
        if i % 2 == 0:
            e = i // 2
            m_ctx, m_lat = attn_delta_mixer(h_ctx, h_lat, hyb_w_in[e], hyb_w_out[e], att_sink[e],
                                            dn_conv[e], dn_a_log[e], dn_dt_bias[e], dn_norm_g[e],
                                            cos, sin, not last)
        else:
            j = i // 2
            m_ctx, m_lat = mamba_mixer(h_ctx, h_lat, ssm_w_in[j], ssm_conv[j], ssm_a_log[j],
                                       ssm_dt_bias[j], ssm_d[j], ssm_norm_g[j], ssm_w_out[j], not last)
        x = x + gt1 * m_lat
        x = x + gt2 * conv_ffn(modulate(rms_norm(x, norm_ffn_g[i]), sh2, sc2),
                               ffn_up[i], ffn_conv[i], ffn_down[i])
        if not last:
            y_ctx = y_ctx + cgt1 * m_ctx
            y_ctx = y_ctx + cgt2 * conv_ffn(modulate(rms_norm(y_ctx, norm_ffn_g[i]), csh2, csc2),
                                            ffn_up[i], ffn_conv[i], ffn_down[i])
    return rms_norm(x, final_g)
```

```python
import functools
import math

import numpy as np
import jax
import jax.numpy as jnp
from jax import lax
from jax.experimental import pallas as pl
from jax.experimental.pallas import tpu as pltpu

f32 = jnp.float32
bf16 = jnp.bfloat16

D_MODEL = 1024
GRID_W = 64
EPS = 1e-6
NEG_INF = -1e30

ATT_HEADS = 8
ATT_KV_HEADS = 2
ATT_GROUP = ATT_HEADS // ATT_KV_HEADS
ATT_HEAD_DIM = 64
ATT_BLOCK = 128
ROPE_BASE = 10000.0
ATT_Q = ATT_HEADS * ATT_HEAD_DIM
ATT_KV = ATT_KV_HEADS * ATT_HEAD_DIM

DN_HEADS = 8
DN_HEAD_DIM = 64
DN_WIDTH = DN_HEADS * DN_HEAD_DIM
DN_CONV = 5
DN_CHUNK = 64

SSM_D_INNER = 2 * D_MODEL
SSM_HEAD_DIM = 64
SSM_HEADS = SSM_D_INNER // SSM_HEAD_DIM
SSM_GROUPS = 4
SSM_HPG = SSM_HEADS // SSM_GROUPS
SSM_STATE = 128
SSM_CONV = 5
SSM_CHUNK = 128
SSM_BC = SSM_GROUPS * SSM_STATE
SSM_CONV_DIM = SSM_D_INNER + 2 * SSM_BC
SSM_GW = SSM_HPG * SSM_HEAD_DIM

FFN_DIM = 2816
FFN_CONV = 3
FFN_CHUNK = 256

LANE = 128
SUBLANE = 8
HALO = SUBLANE
FFN_HALO = 2 * SUBLANE
VMEM_CAP_MB = 56

HI = lax.Precision.HIGHEST


def _cparams(sem, vmem_mb):
    return pltpu.CompilerParams(dimension_semantics=sem,
                                vmem_limit_bytes=min(vmem_mb, VMEM_CAP_MB) * 1024 * 1024)


def _bdot(a, b):
    return jnp.dot(a.astype(bf16), b.astype(bf16), preferred_element_type=f32)


def _bdot_nt(a, b):
    return lax.dot_general(a.astype(bf16), b.astype(bf16), (((1,), (1,)), ((), ())),
                           preferred_element_type=f32)


def _bdot_tn(a, b):
    return lax.dot_general(a.astype(bf16), b.astype(bf16), (((0,), (0,)), ((), ())),
                           preferred_element_type=f32)


def _hdot(a, b):
    return jnp.dot(a, b, precision=HI, preferred_element_type=f32)


def _hdot_nt(a, b):
    return lax.dot_general(a, b, (((1,), (1,)), ((), ())), precision=HI, preferred_element_type=f32)


def _split_bf16(a):
    hi = a.astype(bf16)
    lo = (a - hi.astype(f32)).astype(bf16)
    return hi, lo


def _dot3(a, b):
    ah, al = _split_bf16(a)
    bh, bl = _split_bf16(b)
    d = functools.partial(jnp.dot, preferred_element_type=f32)
    return d(ah, bh) + (d(ah, bl) + d(al, bh))


def _silu(x):
    return x * jax.nn.sigmoid(x)


def _softplus(x):
    return jnp.maximum(x, 0.0) + jnp.log1p(jnp.exp(-jnp.abs(x)))


def _norm_mod(x, g, sh, sc):
    ms = jnp.mean(x * x, axis=-1, keepdims=True)
    return (x * lax.rsqrt(ms + EPS) * g) * (1.0 + sc) + sh


def _iota(shape, dim):
    return lax.broadcasted_iota(jnp.int32, shape, dim)


def _mod_kernel(c_ref, w_ref, b_ref, o_ref):
    o_ref[...] = _bdot(_silu(c_ref[...]), w_ref[...]) + b_ref[...]


def _modulation(cond, w, b):
    m, d = cond.shape
    n = w.shape[1]
    tn = 512
    return pl.pallas_call(
        _mod_kernel,
        grid=(n // tn,),
        in_specs=[pl.BlockSpec((m, d), lambda j: (0, 0)),
                  pl.BlockSpec((d, tn), lambda j: (0, j)),
                  pl.BlockSpec((1, tn), lambda j: (0, j))],
        out_specs=pl.BlockSpec((m, tn), lambda j: (0, j)),
        out_shape=jax.ShapeDtypeStruct((m, n), f32),
        compiler_params=_cparams(("arbitrary",), 24),
        name="modulation",
    )(cond, w, b.reshape(1, n))


def _swap_halves(t):
    w = t.shape[-1]
    first = (_iota(t.shape, 1) % ATT_HEAD_DIM) < (ATT_HEAD_DIM // 2)
    return jnp.where(first, pltpu.roll(t, w - ATT_HEAD_DIM // 2, 1), pltpu.roll(t, ATT_HEAD_DIM // 2, 1))


def _inproj_kernel(*refs, splits, rope, sh_row, sc_row):
    if rope:
        x_ref, g_ref, mp_ref, w_ref, cos_ref, sin_ref = refs[:6]
        outs = refs[6:]
    else:
        x_ref, g_ref, mp_ref, w_ref = refs[:4]
        outs = refs[4:]
    mp = mp_ref[0]
    h = _norm_mod(x_ref[0], g_ref[...], mp[sh_row:sh_row + 1], mp[sc_row:sc_row + 1]).astype(bf16)
    for (start, width, roped), o_ref in zip(splits, outs):
        r = jnp.dot(h, w_ref[:, start:start + width], preferred_element_type=f32)
        if roped:
            reps = width // LANE
            cs = jnp.concatenate([cos_ref[...]] * reps, axis=1)
            sn = jnp.concatenate([sin_ref[...]] * reps, axis=1)
            r = r * cs + _swap_halves(r) * sn
        o_ref[0] = r


def _inproj(x, g, mp, w, splits, tm, rope_tabs=None):
    bsz, length, d = x.shape
    n = w.shape[1]
    per_batch = mp.shape[0] != 1
    rope = rope_tabs is not None
    in_specs = [pl.BlockSpec((1, tm, d), lambda b, t: (b, t, 0)),
                pl.BlockSpec((1, d), lambda b, t: (0, 0)),
                pl.BlockSpec((1, 6, d), (lambda b, t: (b, 0, 0)) if per_batch else (lambda b, t: (0, 0, 0))),
                pl.BlockSpec((d, n), lambda b, t: (0, 0))]
    args = [x, g.reshape(1, d), mp, w]
    if rope:
        in_specs += [pl.BlockSpec((tm, LANE), lambda b, t: (t, 0))] * 2
        args += list(rope_tabs)
    out_w = sum(s[1] for s in splits)
    vmem = (2 * tm * d * 4 + 2 * d * n * 2 + 3 * tm * out_w * 4) // (1 << 20) + 8
    return pl.pallas_call(
        functools.partial(_inproj_kernel, splits=tuple(splits), rope=rope, sh_row=0, sc_row=1),
        grid=(bsz, length // tm),
        in_specs=in_specs,
        out_specs=[pl.BlockSpec((1, tm, s[1]), lambda b, t: (b, t, 0)) for s in splits],
        out_shape=[jax.ShapeDtypeStruct((bsz, length, s[1]), f32) for s in splits],
        compiler_params=_cparams(("parallel", "arbitrary"), vmem),
        name="inproj",
    )(*args)


def _dwconv_silu_kernel(xp_ref, x_ref, xn_ref, w_ref, o_ref, ext_ref, *, taps, nt):
    t = pl.program_id(1)
    tm = x_ref.shape[1]
    ext_ref[0:HALO, :] = jnp.where(t > 0, xp_ref[0], 0.0)
    ext_ref[HALO:HALO + tm, :] = x_ref[0]
    ext_ref[HALO + tm:2 * HALO + tm, :] = jnp.where(t < nt - 1, xn_ref[0], 0.0)
    pad = taps // 2
    acc = w_ref[0:1, :] * ext_ref[pl.ds(HALO - pad, tm), :]
    for k in range(1, taps):
        acc = acc + w_ref[k:k + 1, :] * ext_ref[pl.ds(HALO - pad + k, tm), :]
    o_ref[0] = _silu(acc)


def _dwconv_silu(x, w, tm, cw):
    bsz, length, c = x.shape
    taps = w.shape[0]
    nt = length // tm
    hb = tm // HALO
    last_hb = length // HALO - 1
    return pl.pallas_call(
        functools.partial(_dwconv_silu_kernel, taps=taps, nt=nt),
        grid=(bsz, nt, c // cw),
        in_specs=[pl.BlockSpec((1, HALO, cw), lambda b, t, j: (b, jnp.maximum(t * hb - 1, 0), j)),
                  pl.BlockSpec((1, tm, cw), lambda b, t, j: (b, t, j)),
                  pl.BlockSpec((1, HALO, cw), lambda b, t, j: (b, jnp.minimum((t + 1) * hb, last_hb), j)),
                  pl.BlockSpec((taps, cw), lambda b, t, j: (0, j))],
        out_specs=pl.BlockSpec((1, tm, cw), lambda b, t, j: (b, t, j)),
        out_shape=jax.ShapeDtypeStruct((bsz, length, c), f32),
        scratch_shapes=[pltpu.VMEM((tm + 2 * HALO, cw), f32)],
        compiler_params=_cparams(("parallel", "arbitrary", "arbitrary"), 24),
        name="dwconv_silu",
    )(x, x, x, w)


def _attn_kernel(*refs, has_local, nb):
    if has_local:
        sink_ref, q_ref, kp_ref, ko_ref, kn_ref, kc_ref, o_ref = refs
    else:
        sink_ref, q_ref, kc_ref, o_ref = refs
    i = pl.program_id(1)
    scale = ATT_HEAD_DIM ** -0.5
    q = q_ref[0]
    kvc = kc_ref[0]
    if has_local:
        kvp, kvo, kvn = kp_ref[0], ko_ref[0], kn_ref[0]
        qi = _iota((ATT_BLOCK, ATT_BLOCK), 0)
        kj = _iota((ATT_BLOCK, ATT_BLOCK), 1)
        mask_p = (kj >= qi) & (i >= 1)
        mask_n = (kj <= qi) & (i <= nb - 2)
    hd = ATT_HEAD_DIM
    for kvh in range(ATT_KV_HEADS):
        ks, vs = slice(kvh * hd, (kvh + 1) * hd), slice(ATT_KV + kvh * hd, ATT_KV + (kvh + 1) * hd)
        for gi in range(ATT_GROUP):
            h = kvh * ATT_GROUP + gi
            qh = q[:, h * hd:(h + 1) * hd].astype(bf16)
            sink = sink_ref[h]
            parts = [(_bdot_nt(qh, kvc[:, ks]) * scale, kvc[:, vs])]
            if has_local:
                parts.append((jnp.where(mask_p, _bdot_nt(qh, kvp[:, ks]) * scale, NEG_INF), kvp[:, vs]))
                parts.append((_bdot_nt(qh, kvo[:, ks]) * scale, kvo[:, vs]))
                parts.append((jnp.where(mask_n, _bdot_nt(qh, kvn[:, ks]) * scale, NEG_INF), kvn[:, vs]))
            m = jnp.full((q.shape[0], 1), sink, f32)
            for s, _ in parts:
                m = jnp.maximum(m, jnp.max(s, axis=-1, keepdims=True))
            denom = jnp.exp(sink - m)
            acc = None
            for s, v in parts:
                p = jnp.exp(s - m)
                denom = denom + jnp.sum(p, axis=-1, keepdims=True)
                pv = _bdot(p, v)
                acc = pv if acc is None else acc + pv
            o_ref[0, :, h * hd:(h + 1) * hd] = acc / denom


def _attention(q, kv, kv_ctx, sink, has_local):
    bsz, length, _ = q.shape
    ctx_len = kv_ctx.shape[1]
    nb = length // ATT_BLOCK
    kvw = 2 * ATT_KV
    in_specs = [pl.BlockSpec(memory_space=pltpu.SMEM),
                pl.BlockSpec((1, ATT_BLOCK, ATT_Q), lambda b, i: (b, i, 0))]
    args = [sink, q]
    if has_local:
        in_specs += [pl.BlockSpec((1, ATT_BLOCK, kvw), lambda b, i: (b, jnp.maximum(i - 1, 0), 0)),
                     pl.BlockSpec((1, ATT_BLOCK, kvw), lambda b, i: (b, i, 0)),
                     pl.BlockSpec((1, ATT_BLOCK, kvw), lambda b, i: (b, jnp.minimum(i + 1, nb - 1), 0))]
        args += [kv, kv, kv]
    in_specs.append(pl.BlockSpec((1, ctx_len, kvw), lambda b, i: (b, 0, 0)))
    args.append(kv_ctx)
    return pl.pallas_call(
        functools.partial(_attn_kernel, has_local=has_local, nb=nb),
        grid=(bsz, nb),
        in_specs=in_specs,
        out_specs=pl.BlockSpec((1, ATT_BLOCK, ATT_Q), lambda b, i: (b, i, 0)),
        out_shape=jax.ShapeDtypeStruct((bsz, length, ATT_Q), f32),
        compiler_params=_cparams(("parallel", "arbitrary"), 24),
        name="attention",
    )(*args)


def _inv_unit_tri(a, eye):
    p = eye - a
    ak = a
    for _ in range(5):
        ak = _dot3(ak, ak)
        p = p + _dot3(p, ak)
    return p


def _dn_direction(d, qkv, dba, par_ref, s_ref, o_ref, with_out):
    c = DN_CHUNK
    hd = DN_HEAD_DIM
    row = _iota((c, c), 0)
    col = _iota((c, c), 1)
    if d == 0:
        incl, strict, last = row >= col, row > col, c - 1
    else:
        incl, strict, last = row <= col, row < col, 0
    eye = (row == col).astype(f32)
    sig = jax.nn.sigmoid(dba)
    ld = -jnp.exp(par_ref[0:1, :]) * _softplus(dba + par_ref[1:2, :])
    g_all = _hdot(incl.astype(f32), ld)
    e128 = (_iota((LANE, LANE), 0) == _iota((LANE, LANE), 1)).astype(f32)
    g_all_t = _hdot_nt(e128, g_all)
    for h in range(DN_HEADS):
        cb, cg = d * DN_HEADS + h, 2 * DN_HEADS + d * DN_HEADS + h
        bt = sig[:, cb:cb + 1]
        g = g_all[:, cg:cg + 1]
        g_row = g_all_t[cg:cg + 1, :]
        g_last = g_all[last:last + 1, cg:cg + 1]
        qh = qkv[:, h * hd:(h + 1) * hd]
        kh = qkv[:, DN_WIDTH + h * hd:DN_WIDTH + (h + 1) * hd]
        vh = qkv[:, 2 * DN_WIDTH + h * hd:2 * DN_WIDTH + (h + 1) * hd]
        kn = kh * lax.rsqrt(jnp.sum(kh * kh, axis=-1, keepdims=True) + EPS)
        e_incl = jnp.exp(jnp.where(incl, g - g_row, -jnp.inf))
        eg = jnp.exp(g)
        a = bt * _bdot_nt(kn, kn) * jnp.where(strict, e_incl, 0.0)
        t_inv = _inv_unit_tri(a, eye)
        uw = _dot3(t_inv, jnp.concatenate([bt * vh, (bt * eg) * kn], axis=1))
        u0, wk = uw[:, :hd], uw[:, hd:]
        s = s_ref[0, d, h]
        u = u0 - _bdot(wk, s)
        if with_out:
            qn = qh * lax.rsqrt(jnp.sum(qh * qh, axis=-1, keepdims=True) + EPS) * (hd ** -0.5)
            mq = _bdot_nt(qn, kn) * e_incl
            o_ref[0, :, h * hd:(h + 1) * hd] = _bdot(qn * eg, s) + _bdot(mq, u)
        kd = kn * jnp.exp(g_last - g)
        s_ref[0, d, h] = jnp.exp(g_last) * s + _bdot_tn(kd, u)


def _dn_kernel(*refs, with_out):
    if with_out:
        qf_ref, qb_ref, bf_ref, bb_ref, par_ref, s0_ref, of_ref, ob_ref, s_ref = refs
    else:
        qf_ref, qb_ref, bf_ref, bb_ref, par_ref, s0_ref, s_ref = refs
        of_ref = ob_ref = None

    @pl.when(pl.program_id(1) == 0)
    def _():
        s_ref[...] = s0_ref[...]

    _dn_direction(0, qf_ref[0], bf_ref[0], par_ref, s_ref, of_ref, with_out)
    _dn_direction(1, qb_ref[0], bb_ref[0], par_ref, s_ref, ob_ref, with_out)


def _deltanet(qkv, dba, par, s0, with_out):
    bsz, length, qw = qkv.shape
    nc = length // DN_CHUNK
    c = DN_CHUNK
    st_shape = (bsz, 2, DN_HEADS, DN_HEAD_DIM, DN_HEAD_DIM)
    st_spec = pl.BlockSpec((1,) + st_shape[1:], lambda b, s: (b, 0, 0, 0, 0))
    in_specs = [pl.BlockSpec((1, c, qw), lambda b, s: (b, s, 0)),
                pl.BlockSpec((1, c, qw), lambda b, s: (b, nc - 1 - s, 0)),
                pl.BlockSpec((1, c, LANE), lambda b, s: (b, s, 0)),
                pl.BlockSpec((1, c, LANE), lambda b, s: (b, nc - 1 - s, 0)),
                pl.BlockSpec((2, LANE), lambda b, s: (0, 0)),
                st_spec]
    out_specs, out_shape = [], []
    if with_out:
        out_specs += [pl.BlockSpec((1, c, DN_WIDTH), lambda b, s: (b, s, 0)),
                      pl.BlockSpec((1, c, DN_WIDTH), lambda b, s: (b, nc - 1 - s, 0))]
        out_shape += [jax.ShapeDtypeStruct((bsz, length, DN_WIDTH), f32)] * 2
    out_specs.append(st_spec)
    out_shape.append(jax.ShapeDtypeStruct(st_shape, f32))
    return pl.pallas_call(
        functools.partial(_dn_kernel, with_out=with_out),
        grid=(bsz, nc),
        in_specs=in_specs,
        out_specs=out_specs,
        out_shape=out_shape,
        compiler_params=_cparams(("parallel", "arbitrary"), 24),
        name="deltanet",
    )(qkv, qkv, dba, dba, par, s0)


def _hyb_out_kernel(att_ref, of_ref, ob_ref, z_ref, x_ref, mp_ref, g_ref, bd_ref, w_ref, o_ref):
    dn = of_ref[0] + ob_ref[0]
    ms = _hdot(dn * dn, bd_ref[...])
    gated = (dn * lax.rsqrt(ms + EPS) * g_ref[...]) * _silu(z_ref[0])
    m = _bdot(att_ref[0], w_ref[0:ATT_Q, :]) + _bdot(gated, w_ref[ATT_Q:ATT_Q + DN_WIDTH, :])
    o_ref[0] = x_ref[0] + mp_ref[0][2:3] * m


def _hyb_out(att, o_f, o_b, z, x, mp, dn_norm_g, w_out, tm):
    bsz, length, d = x.shape
    per_batch = mp.shape[0] != 1
    head_of = np.arange(DN_WIDTH) // DN_HEAD_DIM
    bd = jnp.asarray((head_of[:, None] == head_of[None, :]).astype(np.float32) / DN_HEAD_DIM)
    g_row = jnp.tile(dn_norm_g, DN_HEADS).reshape(1, DN_WIDTH)
    tok = lambda w: pl.BlockSpec((1, tm, w), lambda b, t: (b, t, 0))
    const = lambda shape: pl.BlockSpec(shape, lambda b, t: (0,) * len(shape))
    return pl.pallas_call(
        _hyb_out_kernel,
        grid=(bsz, length // tm),
        in_specs=[tok(ATT_Q), tok(DN_WIDTH), tok(DN_WIDTH), tok(DN_WIDTH), tok(d),
                  pl.BlockSpec((1, 6, d), (lambda b, t: (b, 0, 0)) if per_batch else (lambda b, t: (0, 0, 0))),
                  const((1, DN_WIDTH)), const((DN_WIDTH, DN_WIDTH)), const((ATT_Q + DN_WIDTH, d))],
        out_specs=tok(d),
        out_shape=jax.ShapeDtypeStruct((bsz, length, d), f32),
        compiler_params=_cparams(("parallel", "arbitrary"), 40),
        name="hyb_out",
    )(att, o_f, o_b, z, x, mp, g_row, bd, w_out)


def _ssd_out_kernel(yf_ref, yb_ref, xa_ref, xb_ref, z_ref, x_ref, mp_ref, dsk_ref, g_ref, w_ref, o_ref):
    xs = jnp.concatenate([xa_ref[0], xb_ref[0]], axis=1)
    y = (yf_ref[0] + yb_ref[0] + dsk_ref[...] * xs) * _silu(z_ref[0])
    gw = SSM_D_INNER // SSM_GROUPS
    m = None
    for gi in range(SSM_GROUPS):
        yg = y[:, gi * gw:(gi + 1) * gw]
        ms = jnp.mean(yg * yg, axis=-1, keepdims=True)
        part = _bdot(yg * lax.rsqrt(ms + EPS) * g_ref[:, gi * gw:(gi + 1) * gw], w_ref[gi * gw:(gi + 1) * gw, :])
        m = part if m is None else m + part
    o_ref[0] = x_ref[0] + mp_ref[0][2:3] * m


def _ssd_out(y_f, y_b, xbc_act, z, x, mp, d_skip, norm_g, w_out, tm):
    bsz, length, d = x.shape
    di = SSM_D_INNER
    half = di // 2
    tok = lambda w: pl.BlockSpec((1, tm, w), lambda b, t: (b, t, 0))
    const = lambda shape: pl.BlockSpec(shape, lambda b, t: (0,) * len(shape))
    return pl.pallas_call(
        _ssd_out_kernel,
        grid=(bsz, length // tm),
        in_specs=[tok(di), tok(di),
                  pl.BlockSpec((1, tm, half), lambda b, t: (b, t, 0)),
                  pl.BlockSpec((1, tm, half), lambda b, t: (b, t, 1)),
                  tok(di), tok(d),
                  pl.BlockSpec((1, 6, d), lambda b, t: (b, 0, 0)),
                  const((1, di)), const((1, di)), const((di, d))],
        out_specs=tok(d),
        out_shape=jax.ShapeDtypeStruct((bsz, length, d), f32),
        compiler_params=_cparams(("parallel", "arbitrary"), 48),
        name="ssd_out",
    )(y_f, y_b, xbc_act, xbc_act, z, x, mp,
      jnp.repeat(d_skip, SSM_HEAD_DIM).reshape(1, di), norm_g.reshape(1, di), w_out)


def _ffn_kernel(xp_ref, x_ref, xn_ref, g_ref, mp_ref, wv_ref, wg_ref, cv_ref, cg_ref, wd_ref, fg_ref,
                o_ref, h_ref, uv_ref, ug_ref, acc_ref, *, nt, final_norm):
    t = pl.program_id(1)
    j = pl.program_id(2)
    tm = x_ref.shape[1]
    mp = mp_ref[0]

    hl = FFN_HALO

    @pl.when(j == 0)
    def _():
        nm = lambda v: _norm_mod(v, g_ref[...], mp[3:4], mp[4:5])
        h_ref[0:hl, :] = jnp.where(t > 0, nm(xp_ref[0]), 0.0).astype(bf16)
        h_ref[hl:hl + tm, :] = nm(x_ref[0]).astype(bf16)
        h_ref[hl + tm:2 * hl + tm, :] = jnp.where(t < nt - 1, nm(xn_ref[0]), 0.0).astype(bf16)
        acc_ref[...] = jnp.zeros_like(acc_ref)

    h = h_ref[...]
    uv_ref[...] = jnp.dot(h, wv_ref[...], preferred_element_type=f32)
    ug_ref[...] = jnp.dot(h, wg_ref[...], preferred_element_type=f32)
    pad = FFN_CONV // 2
    val = cv_ref[0:1, :] * uv_ref[pl.ds(hl - pad, tm), :]
    gate = cg_ref[0:1, :] * ug_ref[pl.ds(hl - pad, tm), :]
    for k in range(1, FFN_CONV):
        val = val + cv_ref[k:k + 1, :] * uv_ref[pl.ds(hl - pad + k, tm), :]
        gate = gate + cg_ref[k:k + 1, :] * ug_ref[pl.ds(hl - pad + k, tm), :]
    acc_ref[...] += _bdot(_silu(gate) * val, wd_ref[...])

    @pl.when(j == pl.num_programs(2) - 1)
    def _():
        y = x_ref[0] + mp[5:6] * acc_ref[...]
        if final_norm:
            y = y * lax.rsqrt(jnp.mean(y * y, axis=-1, keepdims=True) + EPS) * fg_ref[...]
        o_ref[0] = y


def _conv_ffn(x, g, mp, w_up, w_conv, w_down, tm, final_g=None):
    bsz, length, d = x.shape
    per_batch = mp.shape[0] != 1
    nt = length // tm
    hl = FFN_HALO
    hb = tm // hl
    last_hb = length // hl - 1
    nj = FFN_DIM // FFN_CHUNK
    fc = FFN_CHUNK
    final_norm = final_g is not None
    fg = (final_g if final_norm else g).reshape(1, d)
    vmem = (5 * tm * d * 4 + tm * d * 2 + 2 * tm * fc * 4 + 6 * d * fc * 2) // (1 << 20) + 8
    return pl.pallas_call(
        functools.partial(_ffn_kernel, nt=nt, final_norm=final_norm),
        grid=(bsz, nt, nj),
        in_specs=[pl.BlockSpec((1, hl, d), lambda b, t, j: (b, jnp.maximum(t * hb - 1, 0), 0)),
                  pl.BlockSpec((1, tm, d), lambda b, t, j: (b, t, 0)),
                  pl.BlockSpec((1, hl, d), lambda b, t, j: (b, jnp.minimum((t + 1) * hb, last_hb), 0)),
                  pl.BlockSpec((1, d), lambda b, t, j: (0, 0)),
                  pl.BlockSpec((1, 6, d), (lambda b, t, j: (b, 0, 0)) if per_batch else (lambda b, t, j: (0, 0, 0))),
                  pl.BlockSpec((d, fc), lambda b, t, j: (0, j)),
                  pl.BlockSpec((d, fc), lambda b, t, j: (0, nj + j)),
                  pl.BlockSpec((FFN_CONV, fc), lambda b, t, j: (0, j)),
                  pl.BlockSpec((FFN_CONV, fc), lambda b, t, j: (0, nj + j)),
                  pl.BlockSpec((fc, d), lambda b, t, j: (j, 0)),
                  pl.BlockSpec((1, d), lambda b, t, j: (0, 0))],
        out_specs=pl.BlockSpec((1, tm, d), lambda b, t, j: (b, t, 0)),
        out_shape=jax.ShapeDtypeStruct((bsz, length, d), f32),
        scratch_shapes=[pltpu.VMEM((tm + 2 * hl, d), bf16),
                        pltpu.VMEM((tm + 2 * hl, fc), f32),
                        pltpu.VMEM((tm + 2 * hl, fc), f32),
                        pltpu.VMEM((tm, d), f32)],
        compiler_params=_cparams(("parallel", "arbitrary", "arbitrary"), vmem),
        name="conv_ffn",
    )(x, x, x, g.reshape(1, d), mp, w_up, w_up, w_conv, w_conv, w_down, fg)


def _ssd_direction(d, xbc, dtr, par_ref, s_ref, y_ref, with_y):
    c = SSM_CHUNK
    p = SSM_HEAD_DIM
    row = _iota((c, c), 0)
    col = _iota((c, c), 1)
    if d == 0:
        incl, last = row >= col, c - 1
    else:
        incl, last = row <= col, 0
    dt = _softplus(dtr + par_ref[1:2, :])
    g_all = _hdot(incl.astype(f32), dt * (-jnp.exp(par_ref[0:1, :])))
    if with_y:
        e128 = (_iota((LANE, LANE), 0) == _iota((LANE, LANE), 1)).astype(f32)
        g_all_t = _hdot_nt(e128, g_all)
    g_last_all = g_all[last:last + 1, :]
    w_all = jnp.exp(g_last_all - g_all)
    eg_all = jnp.exp(g_all)
    ge_all = jnp.exp(g_last_all)
    for gi in range(SSM_GROUPS):
        bg = xbc[:, SSM_D_INNER + gi * SSM_STATE:SSM_D_INNER + (gi + 1) * SSM_STATE]
        if with_y:
            cg = xbc[:, SSM_D_INNER + SSM_BC + gi * SSM_STATE:SSM_D_INNER + SSM_BC + (gi + 1) * SSM_STATE]
            scores = _bdot_nt(cg, bg)
            s_in = s_ref[0, d, gi]
            y_inter = _bdot(cg, s_in)
        xw, ge_row = [], []
        for hj in range(SSM_HPG):
            h = gi * SSM_HPG + hj
            cl = d * SSM_HEADS + h
            xdt = xbc[:, h * p:(h + 1) * p] * dt[:, cl:cl + 1]
            xw.append(xdt * w_all[:, cl:cl + 1])
            ge_row.append(jnp.broadcast_to(ge_all[:, cl:cl + 1], (1, p)))
            if with_y:
                decay = jnp.exp(jnp.where(incl, g_all[:, cl:cl + 1] - g_all_t[cl:cl + 1, :], -jnp.inf))
                y_ref[0, :, h * p:(h + 1) * p] = (y_inter[:, hj * p:(hj + 1) * p] * eg_all[:, cl:cl + 1]
                                                  + _bdot(scores * decay, xdt))
        d_state = _bdot_tn(bg, jnp.concatenate(xw, axis=1))
        s_ref[0, d, gi] = jnp.concatenate(ge_row, axis=1) * s_ref[0, d, gi] + d_state


def _ssd_kernel(*refs, with_y):
    if with_y:
        xf_ref, xb_ref, df_ref, db_ref, par_ref, s0_ref, yf_ref, yb_ref, s_ref = refs
    else:
        xf_ref, xb_ref, df_ref, db_ref, par_ref, s0_ref, s_ref = refs
        yf_ref = yb_ref = None

    @pl.when(pl.program_id(1) == 0)
    def _():
        s_ref[...] = s0_ref[...]

    _ssd_direction(0, xf_ref[0], df_ref[0], par_ref, s_ref, yf_ref, with_y)
    _ssd_direction(1, xb_ref[0], db_ref[0], par_ref, s_ref, yb_ref, with_y)


def _ssd(xbc, dtr, par, s0, with_y):
    bsz, length, xw = xbc.shape
    c = SSM_CHUNK
    nc = length // c
    st_shape = (bsz, 2, SSM_GROUPS, SSM_STATE, SSM_GW)
    st_spec = pl.BlockSpec((1,) + st_shape[1:], lambda b, s: (b, 0, 0, 0, 0))
    in_specs = [pl.BlockSpec((1, c, xw), lambda b, s: (b, s, 0)),
                pl.BlockSpec((1, c, xw), lambda b, s: (b, nc - 1 - s, 0)),
                pl.BlockSpec((1, c, LANE), lambda b, s: (b, s, 0)),
                pl.BlockSpec((1, c, LANE), lambda b, s: (b, nc - 1 - s, 0)),
                pl.BlockSpec((2, LANE), lambda b, s: (0, 0)),
                st_spec]
    out_specs, out_shape = [], []
    if with_y:
        out_specs += [pl.BlockSpec((1, c, SSM_D_INNER), lambda b, s: (b, s, 0)),
                      pl.BlockSpec((1, c, SSM_D_INNER), lambda b, s: (b, nc - 1 - s, 0))]
        out_shape += [jax.ShapeDtypeStruct((bsz, length, SSM_D_INNER), f32)] * 2
    out_specs.append(st_spec)
    out_shape.append(jax.ShapeDtypeStruct(st_shape, f32))
    return pl.pallas_call(
        functools.partial(_ssd_kernel, with_y=with_y),
        grid=(bsz, nc),
        in_specs=in_specs,
        out_specs=out_specs,
        out_shape=out_shape,
        compiler_params=_cparams(("parallel", "arbitrary"), 40),
        name="ssd_scan",
    )(xbc, xbc, dtr, dtr, par, s0)


def _rope_tables(length):
    rows = length // GRID_W
    row = jnp.repeat(jnp.arange(rows, dtype=f32), GRID_W)
    col = jnp.tile(jnp.arange(GRID_W, dtype=f32), rows)
    n_freq = ATT_HEAD_DIM // 4
    inv = ROPE_BASE ** (-jnp.arange(n_freq, dtype=f32) / n_freq)
    ang = jnp.concatenate([row[:, None] * inv, col[:, None] * inv], axis=-1)
    cos, sin = jnp.cos(ang), jnp.sin(ang)
    reps = LANE // ATT_HEAD_DIM
    return (jnp.tile(jnp.concatenate([cos, cos], axis=-1), (1, reps)),
            jnp.tile(jnp.concatenate([-sin, sin], axis=-1), (1, reps)))


def _pad_cols(w, n):
    return jnp.pad(w, ((0, 0), (0, n - w.shape[1])))


def _lane_row(v, offset):
    return jnp.pad(v.reshape(1, -1), ((0, 0), (offset, LANE - offset - v.size)))


HYB_SPLITS = ((0, ATT_Q, True), (ATT_Q, 2 * ATT_KV, True), (ATT_Q + 2 * ATT_KV, 3 * DN_WIDTH, False),
              (ATT_Q + 2 * ATT_KV + 3 * DN_WIDTH, DN_WIDTH, False),
              (ATT_Q + 2 * ATT_KV + 4 * DN_WIDTH, LANE, False))
SSM_SPLITS = ((0, SSM_D_INNER, False), (SSM_D_INNER, SSM_CONV_DIM, False),
              (SSM_D_INNER + SSM_CONV_DIM, LANE, False))


def _hybrid_layer(x, y_ctx, mp_lat, mp_ctx, norm_g, w_in, w_out, sink, dn_conv, dn_a_log, dn_dt_bias,
                  dn_norm_g, rope_tabs):
    bsz, length, _ = x.shape
    n_pad = HYB_SPLITS[-1][0] + LANE
    w_in_b = _pad_cols(w_in, n_pad).astype(bf16)
    w_out_b = w_out.astype(bf16)
    lat_splits = ((0, ATT_Q, True), (ATT_Q, ATT_KV, True), (ATT_Q + ATT_KV, ATT_KV, False)) + HYB_SPLITS[2:]
    ctx_splits = tuple((s, w, False) for s, w, _ in lat_splits)
    par = jnp.concatenate([_lane_row(dn_a_log, 2 * DN_HEADS), _lane_row(dn_dt_bias, 2 * DN_HEADS)], axis=0)
    s0 = jnp.zeros((bsz, 2, DN_HEADS, DN_HEAD_DIM, DN_HEAD_DIM), f32)

    q_c, k_c, v_c, dqkv_c, z_c, dba_c = _inproj(y_ctx, norm_g, mp_ctx, w_in_b, ctx_splits, 256)
    q_l, k_l, v_l, dqkv_l, z_l, dba_l = _inproj(x, norm_g, mp_lat, w_in_b, lat_splits, 512, rope_tabs)
    kv_c = jnp.concatenate([k_c, v_c], axis=-1)
    kv_l = jnp.concatenate([k_l, v_l], axis=-1)
    att_c = _attention(q_c, kv_c, kv_c, sink, False)
    att_l = _attention(q_l, kv_l, kv_c, sink, True)
    act_c = _dwconv_silu(dqkv_c, dn_conv, 256, 512)
    act_l = _dwconv_silu(dqkv_l, dn_conv, 512, 512)
    of_c, ob_c, s_c = _deltanet(act_c, dba_c, par, s0, True)
    of_l, ob_l, _ = _deltanet(act_l, dba_l, par, s_c, True)
    y_ctx = _hyb_out(att_c, of_c, ob_c, z_c, y_ctx, mp_ctx, dn_norm_g, w_out_b, 256)
    x = _hyb_out(att_l, of_l, ob_l, z_l, x, mp_lat, dn_norm_g, w_out_b, 512)
    return x, y_ctx


def _mamba_layer(x, y_ctx, mp_lat, mp_ctx, norm_g, w_in, conv_w, a_log, dt_bias, d_skip, ssm_norm_g, w_out):
    bsz = x.shape[0]
    n_pad = SSM_SPLITS[-1][0] + LANE
    w_in_b = _pad_cols(w_in, n_pad).astype(bf16)
    par = jnp.concatenate([_lane_row(a_log, 0), _lane_row(dt_bias, 0)], axis=0)
    s0 = jnp.zeros((bsz, 2, SSM_GROUPS, SSM_STATE, SSM_GW), f32)
    xbc_c, dt_c = _inproj(y_ctx, norm_g, mp_ctx, w_in_b, SSM_SPLITS[1:], 256)
    z_l, xbc_l, dt_l = _inproj(x, norm_g, mp_lat, w_in_b, SSM_SPLITS, 256)
    act_c = _dwconv_silu(xbc_c, conv_w, 256, 512)
    act_l = _dwconv_silu(xbc_l, conv_w, 512, 512)
    (s_c,) = _ssd(act_c, dt_c, par, s0, False)
    y_f, y_b, _ = _ssd(act_l, dt_l, par, s_c, True)
    return _ssd_out(y_f, y_b, act_l, z_l, x, mp_lat, d_skip, ssm_norm_g, w_out.astype(bf16), 256)


def kernel(x, c, ctx, c_ctx, mod_w, mod_b, norm_mix_g, norm_ffn_g, ffn_up, ffn_conv, ffn_down,
           hyb_w_in, hyb_w_out, att_sink, dn_conv, dn_a_log, dn_dt_bias, dn_norm_g,
           ssm_w_in, ssm_conv, ssm_a_log, ssm_dt_bias, ssm_d, ssm_norm_g, ssm_w_out, final_g):
    bsz, length, d = x.shape
    depth = mod_w.shape[0]
    rope_tabs = _rope_tables(length)
    cond = jnp.concatenate([c, c_ctx[None, :], jnp.zeros((2 * SUBLANE - bsz - 1, d), f32)], axis=0)
    y_ctx = ctx
    for i in range(depth):
        last = i == depth - 1
        mods = _modulation(cond, mod_w[i], mod_b[i]).reshape(2 * SUBLANE, 6, d)
        mp_lat, mp_ctx = mods[:bsz], mods[bsz:bsz + 1]
        if i % 2 == 0:
            e = i // 2
            x, m_ctx = _hybrid_layer(x, y_ctx, mp_lat, mp_ctx, norm_mix_g[i], hyb_w_in[e], hyb_w_out[e],
                                     att_sink[e], dn_conv[e], dn_a_log[e], dn_dt_bias[e], dn_norm_g[e], rope_tabs)
        else:
            j = i // 2
            x = _mamba_layer(x, y_ctx, mp_lat, mp_ctx, norm_mix_g[i], ssm_w_in[j], ssm_conv[j], ssm_a_log[j],
                             ssm_dt_bias[j], ssm_d[j], ssm_norm_g[j], ssm_w_out[j])
            m_ctx = None
        w_up, w_down = ffn_up[i].astype(bf16), ffn_down[i].astype(bf16)
        x = _conv_ffn(x, norm_ffn_g[i], mp_lat, w_up, ffn_conv[i], w_down, 512,
                      final_g if last else None)
        if not last:
            y_ctx = _conv_ffn(m_ctx, norm_ffn_g[i], mp_ctx, w_up, ffn_conv[i], w_down, 256)
    return x
```

```python
import functools
import math

import numpy as np
import jax
import jax.numpy as jnp
from jax import lax
from jax.experimental import pallas as pl
from jax.experimental.pallas import tpu as pltpu

f32 = jnp.float32
bf16 = jnp.bfloat16

D_MODEL = 1024
GRID_W = 64
EPS = 1e-6
NEG_INF = -1e30

ATT_HEADS = 8
ATT_KV_HEADS = 2
ATT_GROUP = ATT_HEADS // ATT_KV_HEADS
ATT_HEAD_DIM = 64
ATT_BLOCK = 128
ROPE_BASE = 10000.0
ATT_Q = ATT_HEADS * ATT_HEAD_DIM
ATT_KV = ATT_KV_HEADS * ATT_HEAD_DIM

DN_HEADS = 8
DN_HEAD_DIM = 64
DN_WIDTH = DN_HEADS * DN_HEAD_DIM
DN_CONV = 5
DN_CHUNK = 64

SSM_D_INNER = 2 * D_MODEL
SSM_HEAD_DIM = 64
SSM_HEADS = SSM_D_INNER // SSM_HEAD_DIM
SSM_GROUPS = 4
SSM_HPG = SSM_HEADS // SSM_GROUPS
SSM_STATE = 128
SSM_CONV = 5
SSM_CHUNK = 128
SSM_BC = SSM_GROUPS * SSM_STATE
SSM_CONV_DIM = SSM_D_INNER + 2 * SSM_BC
SSM_GW = SSM_HPG * SSM_HEAD_DIM

FFN_DIM = 2816
FFN_CONV = 3
FFN_CHUNK = 256

LANE = 128
SUBLANE = 8
HALO = SUBLANE
FFN_HALO = 2 * SUBLANE
VMEM_CAP_MB = 56

HI = lax.Precision.HIGHEST


def _cparams(sem, vmem_mb):
    return pltpu.CompilerParams(dimension_semantics=sem,
                                vmem_limit_bytes=min(vmem_mb, VMEM_CAP_MB) * 1024 * 1024)


def _bdot(a, b):
    return jnp.dot(a.astype(bf16), b.astype(bf16), preferred_element_type=f32)


def _bdot_nt(a, b):
    return lax.dot_general(a.astype(bf16), b.astype(bf16), (((1,), (1,)), ((), ())),
                           preferred_element_type=f32)


def _bdot_tn(a, b):
    return lax.dot_general(a.astype(bf16), b.astype(bf16), (((0,), (0,)), ((), ())),
                           preferred_element_type=f32)


def _hdot(a, b):
    return jnp.dot(a, b, precision=HI, preferred_element_type=f32)


def _hdot_nt(a, b):
    return lax.dot_general(a, b, (((1,), (1,)), ((), ())), precision=HI, preferred_element_type=f32)


def _split_bf16(a):
    hi = a.astype(bf16)
    lo = (a - hi.astype(f32)).astype(bf16)
    return hi, lo


def _dot3(a, b):
    ah, al = _split_bf16(a)
    bh, bl = _split_bf16(b)
    d = functools.partial(jnp.dot, preferred_element_type=f32)
    return d(ah, bh) + (d(ah, bl) + d(al, bh))


def _silu(x):
    return x * jax.nn.sigmoid(x)


def _softplus(x):
    return jnp.maximum(x, 0.0) + jnp.log1p(jnp.exp(-jnp.abs(x)))


def _norm_mod(x, g, sh, sc):
    ms = jnp.mean(x * x, axis=-1, keepdims=True)
    return (x * lax.rsqrt(ms + EPS) * g) * (1.0 + sc) + sh


def _iota(shape, dim):
    return lax.broadcasted_iota(jnp.int32, shape, dim)


def _mod_kernel(c_ref, w_ref, b_ref, o_ref):
    o_ref[...] = _bdot(_silu(c_ref[...]), w_ref[...]) + b_ref[...]


def _modulation(cond, w, b):
    m, d = cond.shape
    n = w.shape[1]
    tn = 512
    return pl.pallas_call(
        _mod_kernel,
        grid=(n // tn,),
        in_specs=[pl.BlockSpec((m, d), lambda j: (0, 0)),
                  pl.BlockSpec((d, tn), lambda j: (0, j)),
                  pl.BlockSpec((1, tn), lambda j: (0, j))],
        out_specs=pl.BlockSpec((m, tn), lambda j: (0, j)),
        out_shape=jax.ShapeDtypeStruct((m, n), f32),
        compiler_params=_cparams(("arbitrary",), 24),
        name="modulation",
    )(cond, w, b.reshape(1, n))


def _swap_halves(t):
    w = t.shape[-1]
    first = (_iota(t.shape, 1) % ATT_HEAD_DIM) < (ATT_HEAD_DIM // 2)
    return jnp.where(first, pltpu.roll(t, w - ATT_HEAD_DIM // 2, 1), pltpu.roll(t, ATT_HEAD_DIM // 2, 1))


def _inproj_kernel(*refs, splits, rope, sh_row, sc_row):
    if rope:
        x_ref, g_ref, mp_ref, w_ref, cos_ref, sin_ref = refs[:6]
        outs = refs[6:]
    else:
        x_ref, g_ref, mp_ref, w_ref = refs[:4]
        outs = refs[4:]
    mp = mp_ref[0]
    h = _norm_mod(x_ref[0], g_ref[...], mp[sh_row:sh_row + 1], mp[sc_row:sc_row + 1]).astype(bf16)
    for (start, width, roped), o_ref in zip(splits, outs):
        r = jnp.dot(h, w_ref[:, start:start + width], preferred_element_type=f32)
        if roped:
            reps = width // LANE
            cs = jnp.concatenate([cos_ref[...]] * reps, axis=1)
            sn = jnp.concatenate([sin_ref[...]] * reps, axis=1)
            r = r * cs + _swap_halves(r) * sn
        o_ref[0] = r


def _inproj(x, g, mp, w, splits, tm, rope_tabs=None):
    bsz, length, d = x.shape
    n = w.shape[1]
    per_batch = mp.shape[0] != 1
    rope = rope_tabs is not None
    in_specs = [pl.BlockSpec((1, tm, d), lambda b, t: (b, t, 0)),
                pl.BlockSpec((1, d), lambda b, t: (0, 0)),
                pl.BlockSpec((1, 6, d), (lambda b, t: (b, 0, 0)) if per_batch else (lambda b, t: (0, 0, 0))),
                pl.BlockSpec((d, n), lambda b, t: (0, 0))]
    args = [x, g.reshape(1, d), mp, w]
    if rope:
        in_specs += [pl.BlockSpec((tm, LANE), lambda b, t: (t, 0))] * 2
        args += list(rope_tabs)
    out_w = sum(s[1] for s in splits)
    vmem = (2 * tm * d * 4 + 2 * d * n * 2 + 3 * tm * out_w * 4) // (1 << 20) + 8
    return pl.pallas_call(
        functools.partial(_inproj_kernel, splits=tuple(splits), rope=rope, sh_row=0, sc_row=1),
        grid=(bsz, length // tm),
        in_specs=in_specs,
        out_specs=[pl.BlockSpec((1, tm, s[1]), lambda b, t: (b, t, 0)) for s in splits],
        out_shape=[jax.ShapeDtypeStruct((bsz, length, s[1]), f32) for s in splits],
        compiler_params=_cparams(("parallel", "arbitrary"), vmem),
        name="inproj",
    )(*args)


def _dwconv_silu_kernel(xp_ref, x_ref, xn_ref, w_ref, o_ref, ext_ref, *, taps, nt):
    t = pl.program_id(1)
    tm = x_ref.shape[1]
    ext_ref[0:HALO, :] = jnp.where(t > 0, xp_ref[0], 0.0)
    ext_ref[HALO:HALO + tm, :] = x_ref[0]
    ext_ref[HALO + tm:2 * HALO + tm, :] = jnp.where(t < nt - 1, xn_ref[0], 0.0)
    pad = taps // 2
    acc = w_ref[0:1, :] * ext_ref[pl.ds(HALO - pad, tm), :]
    for k in range(1, taps):
        acc = acc + w_ref[k:k + 1, :] * ext_ref[pl.ds(HALO - pad + k, tm), :]
    o_ref[0] = _silu(acc)


def _dwconv_silu(x, w, tm, cw):
    bsz, length, c = x.shape
    taps = w.shape[0]
    nt = length // tm
    hb = tm // HALO
    last_hb = length // HALO - 1
    return pl.pallas_call(
        functools.partial(_dwconv_silu_kernel, taps=taps, nt=nt),
        grid=(bsz, nt, c // cw),
        in_specs=[pl.BlockSpec((1, HALO, cw), lambda b, t, j: (b, jnp.maximum(t * hb - 1, 0), j)),
                  pl.BlockSpec((1, tm, cw), lambda b, t, j: (b, t, j)),
                  pl.BlockSpec((1, HALO, cw), lambda b, t, j: (b, jnp.minimum((t + 1) * hb, last_hb), j)),
                  pl.BlockSpec((taps, cw), lambda b, t, j: (0, j))],
        out_specs=pl.BlockSpec((1, tm, cw), lambda b, t, j: (b, t, j)),
        out_shape=jax.ShapeDtypeStruct((bsz, length, c), f32),
        scratch_shapes=[pltpu.VMEM((tm + 2 * HALO, cw), f32)],
        compiler_params=_cparams(("parallel", "arbitrary", "arbitrary"), 24),
        name="dwconv_silu",
    )(x, x, x, w)


def _attn_kernel(*refs, has_local, nb):
    if has_local:
        sink_ref, q_ref, kp_ref, ko_ref, kn_ref, kc_ref, o_ref = refs
    else:
        sink_ref, q_ref, kc_ref, o_ref = refs
    i = pl.program_id(1)
    scale = ATT_HEAD_DIM ** -0.5
    q = q_ref[0]
    kvc = kc_ref[0]
    if has_local:
        kvp, kvo, kvn = kp_ref[0], ko_ref[0], kn_ref[0]
        qi = _iota((ATT_BLOCK, ATT_BLOCK), 0)
        kj = _iota((ATT_BLOCK, ATT_BLOCK), 1)
        mask_p = (kj >= qi) & (i >= 1)
        mask_n = (kj <= qi) & (i <= nb - 2)
    hd = ATT_HEAD_DIM
    for kvh in range(ATT_KV_HEADS):
        ks, vs = slice(kvh * hd, (kvh + 1) * hd), slice(ATT_KV + kvh * hd, ATT_KV + (kvh + 1) * hd)
        for gi in range(ATT_GROUP):
            h = kvh * ATT_GROUP + gi
            qh = q[:, h * hd:(h + 1) * hd].astype(bf16)
            sink = sink_ref[h]
            parts = [(_bdot_nt(qh, kvc[:, ks]) * scale, kvc[:, vs])]
            if has_local:
                parts.append((jnp.where(mask_p, _bdot_nt(qh, kvp[:, ks]) * scale, NEG_INF), kvp[:, vs]))
                parts.append((_bdot_nt(qh, kvo[:, ks]) * scale, kvo[:, vs]))
                parts.append((jnp.where(mask_n, _bdot_nt(qh, kvn[:, ks]) * scale, NEG_INF), kvn[:, vs]))
            m = jnp.full((q.shape[0], 1), sink, f32)
            for s, _ in parts:
                m = jnp.maximum(m, jnp.max(s, axis=-1, keepdims=True))
            denom = jnp.exp(sink - m)
            acc = None
            for s, v in parts:
                p = jnp.exp(s - m)
                denom = denom + jnp.sum(p, axis=-1, keepdims=True)
                pv = _bdot(p, v)
                acc = pv if acc is None else acc + pv
            o_ref[0, :, h * hd:(h + 1) * hd] = acc / denom


def _attention(q, kv, kv_ctx, sink, has_local):
    bsz, length, _ = q.shape
    ctx_len = kv_ctx.shape[1]
    nb = length // ATT_BLOCK
    kvw = 2 * ATT_KV
    in_specs = [pl.BlockSpec(memory_space=pltpu.SMEM),
                pl.BlockSpec((1, ATT_BLOCK, ATT_Q), lambda b, i: (b, i, 0))]
    args = [sink, q]
    if has_local:
        in_specs += [pl.BlockSpec((1, ATT_BLOCK, kvw), lambda b, i: (b, jnp.maximum(i - 1, 0), 0)),
                     pl.BlockSpec((1, ATT_BLOCK, kvw), lambda b, i: (b, i, 0)),
                     pl.BlockSpec((1, ATT_BLOCK, kvw), lambda b, i: (b, jnp.minimum(i + 1, nb - 1), 0))]
        args += [kv, kv, kv]
    in_specs.append(pl.BlockSpec((1, ctx_len, kvw), lambda b, i: (b, 0, 0)))
    args.append(kv_ctx)
    return pl.pallas_call(
        functools.partial(_attn_kernel, has_local=has_local, nb=nb),
        grid=(bsz, nb),
        in_specs=in_specs,
        out_specs=pl.BlockSpec((1, ATT_BLOCK, ATT_Q), lambda b, i: (b, i, 0)),
        out_shape=jax.ShapeDtypeStruct((bsz, length, ATT_Q), f32),
        compiler_params=_cparams(("parallel", "arbitrary"), 24),
        name="attention",
    )(*args)


def _dn_step(qkvs, dbas, par_ref, s_ref, o_refs, with_out):
    c = DN_CHUNK
    hd = DN_HEAD_DIM
    row = _iota((c, c), 0)
    col = _iota((c, c), 1)
    eye = (row == col).astype(f32)
    incl = (row >= col, row <= col)
    strict = (row > col, row < col)
    last = (c - 1, 0)
    probs = [(d, h) for d in range(2) for h in range(DN_HEADS)]

    sig, g_all, g_all_t = [], [], []
    for d in range(2):
        sig.append(jax.nn.sigmoid(dbas[d]))
        ld = -jnp.exp(par_ref[0:1, :]) * _softplus(dbas[d] + par_ref[1:2, :])
        tri = incl[d].astype(f32)
        g_all.append(_hdot(tri, ld))
        g_all_t.append(lax.dot_general(ld, tri, (((0,), (1,)), ((), ())), precision=HI,
                                       preferred_element_type=f32))

    bt, eg, e_incl, kn, qn, rhs, kd, ge = {}, {}, {}, {}, {}, {}, {}, {}
    for d, h in probs:
        cb, cg = d * DN_HEADS + h, 2 * DN_HEADS + d * DN_HEADS + h
        g = g_all[d][:, cg:cg + 1]
        g_last = g_all[d][last[d]:last[d] + 1, cg:cg + 1]
        kh = qkvs[d][:, DN_WIDTH + h * hd:DN_WIDTH + (h + 1) * hd]
        vh = qkvs[d][:, 2 * DN_WIDTH + h * hd:2 * DN_WIDTH + (h + 1) * hd]
        p = (d, h)
        bt[p] = sig[d][:, cb:cb + 1]
        eg[p] = jnp.exp(g)
        e_incl[p] = jnp.exp(jnp.where(incl[d], g - g_all_t[d][cg:cg + 1, :], -jnp.inf))
        kn[p] = kh * lax.rsqrt(jnp.sum(kh * kh, axis=-1, keepdims=True) + EPS)
        rhs[p] = jnp.concatenate([bt[p] * vh, (bt[p] * eg[p]) * kn[p]], axis=1)
        kd[p] = kn[p] * jnp.exp(g_last - g)
        ge[p] = jnp.exp(g_last)
        if with_out:
            qh = qkvs[d][:, h * hd:(h + 1) * hd]
            qn[p] = qh * lax.rsqrt(jnp.sum(qh * qh, axis=-1, keepdims=True) + EPS) * (hd ** -0.5)

    if with_out:
        qk = {p: _bdot_nt(jnp.concatenate([kn[p], qn[p]], axis=0), kn[p]) for p in probs}
    else:
        qk = {p: _bdot_nt(kn[p], kn[p]) for p in probs}
    a = {p: bt[p] * qk[p][:c] * jnp.where(strict[p[0]], e_incl[p], 0.0) for p in probs}

    t_inv = {p: eye - a[p] for p in probs}
    ak = {p: _dot3(a[p], a[p]) for p in probs}
    for _ in range(4):
        r = {p: _dot3(jnp.concatenate([t_inv[p], ak[p]], axis=0), ak[p]) for p in probs}
        t_inv = {p: t_inv[p] + r[p][:c] for p in probs}
        ak = {p: r[p][c:] for p in probs}
    r = {p: _dot3(t_inv[p], ak[p]) for p in probs}
    t_inv = {p: t_inv[p] + r[p] for p in probs}

    uw = {p: _dot3(t_inv[p], rhs[p]) for p in probs}
    s = {p: s_ref[0, p[0], p[1]] for p in probs}
    u = {p: uw[p][:, :hd] - _bdot(uw[p][:, hd:], s[p]) for p in probs}
    if with_out:
        for p in probs:
            d, h = p
            lhs = jnp.concatenate([qn[p] * eg[p], qk[p][c:] * e_incl[p]], axis=1)
            o_refs[d][0, :, h * hd:(h + 1) * hd] = _bdot(lhs, jnp.concatenate([s[p], u[p]], axis=0))
    for p in probs:
        s_ref[0, p[0], p[1]] = ge[p] * s[p] + _bdot_tn(kd[p], u[p])


def _dn_kernel(*refs, with_out):
    if with_out:
        qf_ref, qb_ref, bf_ref, bb_ref, par_ref, s0_ref, of_ref, ob_ref, s_ref = refs
    else:
        qf_ref, qb_ref, bf_ref, bb_ref, par_ref, s0_ref, s_ref = refs
        of_ref = ob_ref = None

    @pl.when(pl.program_id(1) == 0)
    def _():
        s_ref[...] = s0_ref[...]

    _dn_step((qf_ref[0], qb_ref[0]), (bf_ref[0], bb_ref[0]), par_ref, s_ref, (of_ref, ob_ref), with_out)


def _deltanet(qkv, dba, par, s0, with_out):
    bsz, length, qw = qkv.shape
    nc = length // DN_CHUNK
    c = DN_CHUNK
    st_shape = (bsz, 2, DN_HEADS, DN_HEAD_DIM, DN_HEAD_DIM)
    st_spec = pl.BlockSpec((1,) + st_shape[1:], lambda b, s: (b, 0, 0, 0, 0))
    in_specs = [pl.BlockSpec((1, c, qw), lambda b, s: (b, s, 0)),
                pl.BlockSpec((1, c, qw), lambda b, s: (b, nc - 1 - s, 0)),
                pl.BlockSpec((1, c, LANE), lambda b, s: (b, s, 0)),
                pl.BlockSpec((1, c, LANE), lambda b, s: (b, nc - 1 - s, 0)),
                pl.BlockSpec((2, LANE), lambda b, s: (0, 0)),
                st_spec]
    out_specs, out_shape = [], []
    if with_out:
        out_specs += [pl.BlockSpec((1, c, DN_WIDTH), lambda b, s: (b, s, 0)),
                      pl.BlockSpec((1, c, DN_WIDTH), lambda b, s: (b, nc - 1 - s, 0))]
        out_shape += [jax.ShapeDtypeStruct((bsz, length, DN_WIDTH), f32)] * 2
    out_specs.append(st_spec)
    out_shape.append(jax.ShapeDtypeStruct(st_shape, f32))
    return pl.pallas_call(
        functools.partial(_dn_kernel, with_out=with_out),
        grid=(bsz, nc),
        in_specs=in_specs,
        out_specs=out_specs,
        out_shape=out_shape,
        compiler_params=_cparams(("parallel", "arbitrary"), 24),
        name="deltanet",
    )(qkv, qkv, dba, dba, par, s0)


def _hyb_out_kernel(att_ref, of_ref, ob_ref, z_ref, x_ref, mp_ref, g_ref, bd_ref, w_ref, o_ref):
    dn = of_ref[0] + ob_ref[0]
    ms = _hdot(dn * dn, bd_ref[...])
    gated = (dn * lax.rsqrt(ms + EPS) * g_ref[...]) * _silu(z_ref[0])
    m = _bdot(att_ref[0], w_ref[0:ATT_Q, :]) + _bdot(gated, w_ref[ATT_Q:ATT_Q + DN_WIDTH, :])
    o_ref[0] = x_ref[0] + mp_ref[0][2:3] * m


def _hyb_out(att, o_f, o_b, z, x, mp, dn_norm_g, w_out, tm):
    bsz, length, d = x.shape
    per_batch = mp.shape[0] != 1
    head_of = np.arange(DN_WIDTH) // DN_HEAD_DIM
    bd = jnp.asarray((head_of[:, None] == head_of[None, :]).astype(np.float32) / DN_HEAD_DIM)
    g_row = jnp.tile(dn_norm_g, DN_HEADS).reshape(1, DN_WIDTH)
    tok = lambda w: pl.BlockSpec((1, tm, w), lambda b, t: (b, t, 0))
    const = lambda shape: pl.BlockSpec(shape, lambda b, t: (0,) * len(shape))
    return pl.pallas_call(
        _hyb_out_kernel,
        grid=(bsz, length // tm),
        in_specs=[tok(ATT_Q), tok(DN_WIDTH), tok(DN_WIDTH), tok(DN_WIDTH), tok(d),
                  pl.BlockSpec((1, 6, d), (lambda b, t: (b, 0, 0)) if per_batch else (lambda b, t: (0, 0, 0))),
                  const((1, DN_WIDTH)), const((DN_WIDTH, DN_WIDTH)), const((ATT_Q + DN_WIDTH, d))],
        out_specs=tok(d),
        out_shape=jax.ShapeDtypeStruct((bsz, length, d), f32),
        compiler_params=_cparams(("parallel", "arbitrary"), 40),
        name="hyb_out",
    )(att, o_f, o_b, z, x, mp, g_row, bd, w_out)


def _ssd_out_kernel(yf_ref, yb_ref, xa_ref, xb_ref, z_ref, x_ref, mp_ref, dsk_ref, g_ref, w_ref, o_ref):
    xs = jnp.concatenate([xa_ref[0], xb_ref[0]], axis=1)
    y = (yf_ref[0] + yb_ref[0] + dsk_ref[...] * xs) * _silu(z_ref[0])
    gw = SSM_D_INNER // SSM_GROUPS
    m = None
    for gi in range(SSM_GROUPS):
        yg = y[:, gi * gw:(gi + 1) * gw]
        ms = jnp.mean(yg * yg, axis=-1, keepdims=True)
        part = _bdot(yg * lax.rsqrt(ms + EPS) * g_ref[:, gi * gw:(gi + 1) * gw], w_ref[gi * gw:(gi + 1) * gw, :])
        m = part if m is None else m + part
    o_ref[0] = x_ref[0] + mp_ref[0][2:3] * m


def _ssd_out(y_f, y_b, xbc_act, z, x, mp, d_skip, norm_g, w_out, tm):
    bsz, length, d = x.shape
    di = SSM_D_INNER
    half = di // 2
    tok = lambda w: pl.BlockSpec((1, tm, w), lambda b, t: (b, t, 0))
    const = lambda shape: pl.BlockSpec(shape, lambda b, t: (0,) * len(shape))
    return pl.pallas_call(
        _ssd_out_kernel,
        grid=(bsz, length // tm),
        in_specs=[tok(di), tok(di),
                  pl.BlockSpec((1, tm, half), lambda b, t: (b, t, 0)),
                  pl.BlockSpec((1, tm, half), lambda b, t: (b, t, 1)),
                  tok(di), tok(d),
                  pl.BlockSpec((1, 6, d), lambda b, t: (b, 0, 0)),
                  const((1, di)), const((1, di)), const((di, d))],
        out_specs=tok(d),
        out_shape=jax.ShapeDtypeStruct((bsz, length, d), f32),
        compiler_params=_cparams(("parallel", "arbitrary"), 48),
        name="ssd_out",
    )(y_f, y_b, xbc_act, xbc_act, z, x, mp,
      jnp.repeat(d_skip, SSM_HEAD_DIM).reshape(1, di), norm_g.reshape(1, di), w_out)


def _ffn_kernel(xp_ref, x_ref, xn_ref, g_ref, mp_ref, wv_ref, wg_ref, cv_ref, cg_ref, wd_ref, fg_ref,
                o_ref, h_ref, uv_ref, ug_ref, acc_ref, *, nt, final_norm):
    t = pl.program_id(1)
    j = pl.program_id(2)
    tm = x_ref.shape[1]
    mp = mp_ref[0]

    hl = FFN_HALO

    @pl.when(j == 0)
    def _():
        nm = lambda v: _norm_mod(v, g_ref[...], mp[3:4], mp[4:5])
        h_ref[0:hl, :] = jnp.where(t > 0, nm(xp_ref[0]), 0.0).astype(bf16)
        h_ref[hl:hl + tm, :] = nm(x_ref[0]).astype(bf16)
        h_ref[hl + tm:2 * hl + tm, :] = jnp.where(t < nt - 1, nm(xn_ref[0]), 0.0).astype(bf16)
        acc_ref[...] = jnp.zeros_like(acc_ref)

    h = h_ref[...]
    uv_ref[...] = jnp.dot(h, wv_ref[...], preferred_element_type=f32)
    ug_ref[...] = jnp.dot(h, wg_ref[...], preferred_element_type=f32)
    pad = FFN_CONV // 2
    val = cv_ref[0:1, :] * uv_ref[pl.ds(hl - pad, tm), :]
    gate = cg_ref[0:1, :] * ug_ref[pl.ds(hl - pad, tm), :]
    for k in range(1, FFN_CONV):
        val = val + cv_ref[k:k + 1, :] * uv_ref[pl.ds(hl - pad + k, tm), :]
        gate = gate + cg_ref[k:k + 1, :] * ug_ref[pl.ds(hl - pad + k, tm), :]
    acc_ref[...] += _bdot(_silu(gate) * val, wd_ref[...])

    @pl.when(j == pl.num_programs(2) - 1)
    def _():
        y = x_ref[0] + mp[5:6] * acc_ref[...]
        if final_norm:
            y = y * lax.rsqrt(jnp.mean(y * y, axis=-1, keepdims=True) + EPS) * fg_ref[...]
        o_ref[0] = y


def _conv_ffn(x, g, mp, w_up, w_conv, w_down, tm, final_g=None):
    bsz, length, d = x.shape
    per_batch = mp.shape[0] != 1
    nt = length // tm
    hl = FFN_HALO
    hb = tm // hl
    last_hb = length // hl - 1
    nj = FFN_DIM // FFN_CHUNK
    fc = FFN_CHUNK
    final_norm = final_g is not None
    fg = (final_g if final_norm else g).reshape(1, d)
    vmem = (5 * tm * d * 4 + tm * d * 2 + 2 * tm * fc * 4 + 6 * d * fc * 2) // (1 << 20) + 8
    return pl.pallas_call(
        functools.partial(_ffn_kernel, nt=nt, final_norm=final_norm),
        grid=(bsz, nt, nj),
        in_specs=[pl.BlockSpec((1, hl, d), lambda b, t, j: (b, jnp.maximum(t * hb - 1, 0), 0)),
                  pl.BlockSpec((1, tm, d), lambda b, t, j: (b, t, 0)),
                  pl.BlockSpec((1, hl, d), lambda b, t, j: (b, jnp.minimum((t + 1) * hb, last_hb), 0)),
                  pl.BlockSpec((1, d), lambda b, t, j: (0, 0)),
                  pl.BlockSpec((1, 6, d), (lambda b, t, j: (b, 0, 0)) if per_batch else (lambda b, t, j: (0, 0, 0))),
                  pl.BlockSpec((d, fc), lambda b, t, j: (0, j)),
                  pl.BlockSpec((d, fc), lambda b, t, j: (0, nj + j)),
                  pl.BlockSpec((FFN_CONV, fc), lambda b, t, j: (0, j)),
                  pl.BlockSpec((FFN_CONV, fc), lambda b, t, j: (0, nj + j)),
                  pl.BlockSpec((fc, d), lambda b, t, j: (j, 0)),
                  pl.BlockSpec((1, d), lambda b, t, j: (0, 0))],
        out_specs=pl.BlockSpec((1, tm, d), lambda b, t, j: (b, t, 0)),
        out_shape=jax.ShapeDtypeStruct((bsz, length, d), f32),
        scratch_shapes=[pltpu.VMEM((tm + 2 * hl, d), bf16),
                        pltpu.VMEM((tm + 2 * hl, fc), f32),
                        pltpu.VMEM((tm + 2 * hl, fc), f32),
                        pltpu.VMEM((tm, d), f32)],
        compiler_params=_cparams(("parallel", "arbitrary", "arbitrary"), vmem),
        name="conv_ffn",
    )(x, x, x, g.reshape(1, d), mp, w_up, w_up, w_conv, w_conv, w_down, fg)


def _ssd_step(xbcs, dtrs, par_ref, s_ref, y_refs, with_y):
    c = SSM_CHUNK
    p = SSM_HEAD_DIM
    n = SSM_STATE
    row = _iota((c, c), 0)
    col = _iota((c, c), 1)
    incl = (row >= col, row <= col)
    last = (c - 1, 0)
    a_row = -jnp.exp(par_ref[0:1, :])
    g, g_t, dt_t, wdt_t, ge = [], [], [], [], []
    for d in range(2):
        dt = _softplus(dtrs[d] + par_ref[1:2, :])
        da = dt * a_row
        tri = incl[d].astype(f32)
        g.append(_hdot(tri, da))
        g_t.append(lax.dot_general(da, tri, (((0,), (1,)), ((), ())), precision=HI,
                                   preferred_element_type=f32))
        dt_t.append(dt.T)
        wdt_t.append(jnp.exp(g_t[d][:, last[d]:last[d] + 1] - g_t[d]) * dt_t[d])
        ge.append(jnp.exp(g[d][last[d]:last[d] + 1, :]))

    groups = [(d, gi) for d in range(2) for gi in range(SSM_GROUPS)]
    s_in, b_t, scores, y_inter = {}, {}, {}, {}
    for d, gi in groups:
        bg = xbcs[d][:, SSM_D_INNER + gi * n:SSM_D_INNER + (gi + 1) * n]
        s_in[d, gi] = s_ref[0, d, gi]
        b_t[d, gi] = bg.T
        if with_y:
            cg = xbcs[d][:, SSM_D_INNER + SSM_BC + gi * n:SSM_D_INNER + SSM_BC + (gi + 1) * n]
            scores[d, gi] = _bdot_nt(cg, bg)
            y_inter[d, gi] = _bdot(cg, s_in[d, gi])

    for d, gi in groups:
        for hj in range(SSM_HPG):
            h = gi * SSM_HPG + hj
            cl = d * SSM_HEADS + h
            hs = slice(hj * p, (hj + 1) * p)
            xs = xbcs[d][:, h * p:(h + 1) * p]
            bw = b_t[d, gi] * wdt_t[d][cl:cl + 1, :]
            if with_y:
                gc = jnp.broadcast_to(g[d][:, cl:cl + 1], (c, c))
                dec = jnp.exp(jnp.where(incl[d], gc - g_t[d][cl:cl + 1, :], -jnp.inf))
                m = scores[d, gi] * dec * dt_t[d][cl:cl + 1, :]
                r = _bdot(jnp.concatenate([m, bw], axis=0), xs)
                y_refs[d][0, :, h * p:(h + 1) * p] = y_inter[d, gi][:, hs] * jnp.exp(gc[:, :p]) + r[:c]
                ds = r[c:]
            else:
                ds = _bdot(bw, xs)
            s_ref[0, d, gi, :, hs] = ge[d][:, cl:cl + 1] * s_in[d, gi][:, hs] + ds


def _ssd_kernel(*refs, with_y):
    if with_y:
        xf_ref, xb_ref, df_ref, db_ref, par_ref, s0_ref, yf_ref, yb_ref, s_ref = refs
    else:
        xf_ref, xb_ref, df_ref, db_ref, par_ref, s0_ref, s_ref = refs
        yf_ref = yb_ref = None

    @pl.when(pl.program_id(1) == 0)
    def _():
        s_ref[...] = s0_ref[...]

    _ssd_step((xf_ref[0], xb_ref[0]), (df_ref[0], db_ref[0]), par_ref, s_ref, (yf_ref, yb_ref), with_y)


def _ssd(xbc, dtr, par, s0, with_y):
    bsz, length, xw = xbc.shape
    c = SSM_CHUNK
    nc = length // c
    st_shape = (bsz, 2, SSM_GROUPS, SSM_STATE, SSM_GW)
    st_spec = pl.BlockSpec((1,) + st_shape[1:], lambda b, s: (b, 0, 0, 0, 0))
    in_specs = [pl.BlockSpec((1, c, xw), lambda b, s: (b, s, 0)),
                pl.BlockSpec((1, c, xw), lambda b, s: (b, nc - 1 - s, 0)),
                pl.BlockSpec((1, c, LANE), lambda b, s: (b, s, 0)),
                pl.BlockSpec((1, c, LANE), lambda b, s: (b, nc - 1 - s, 0)),
                pl.BlockSpec((2, LANE), lambda b, s: (0, 0)),
                st_spec]
    out_specs, out_shape = [], []
    if with_y:
        out_specs += [pl.BlockSpec((1, c, SSM_D_INNER), lambda b, s: (b, s, 0)),
                      pl.BlockSpec((1, c, SSM_D_INNER), lambda b, s: (b, nc - 1 - s, 0))]
        out_shape += [jax.ShapeDtypeStruct((bsz, length, SSM_D_INNER), f32)] * 2
    out_specs.append(st_spec)
    out_shape.append(jax.ShapeDtypeStruct(st_shape, f32))
    return pl.pallas_call(
        functools.partial(_ssd_kernel, with_y=with_y),
        grid=(bsz, nc),
        in_specs=in_specs,
        out_specs=out_specs,
        out_shape=out_shape,
        compiler_params=_cparams(("parallel", "arbitrary"), 40),
        name="ssd_scan",
    )(xbc, xbc, dtr, dtr, par, s0)


def _rope_tables(length):
    rows = length // GRID_W
    row = jnp.repeat(jnp.arange(rows, dtype=f32), GRID_W)
    col = jnp.tile(jnp.arange(GRID_W, dtype=f32), rows)
    n_freq = ATT_HEAD_DIM // 4
    inv = ROPE_BASE ** (-jnp.arange(n_freq, dtype=f32) / n_freq)
    ang = jnp.concatenate([row[:, None] * inv, col[:, None] * inv], axis=-1)
    cos, sin = jnp.cos(ang), jnp.sin(ang)
    reps = LANE // ATT_HEAD_DIM
    return (jnp.tile(jnp.concatenate([cos, cos], axis=-1), (1, reps)),
            jnp.tile(jnp.concatenate([-sin, sin], axis=-1), (1, reps)))


def _pad_cols(w, n):
    return jnp.pad(w, ((0, 0), (0, n - w.shape[1])))


def _lane_row(v, offset):
    return jnp.pad(v.reshape(1, -1), ((0, 0), (offset, LANE - offset - v.size)))


HYB_SPLITS = ((0, ATT_Q, True), (ATT_Q, 2 * ATT_KV, True), (ATT_Q + 2 * ATT_KV, 3 * DN_WIDTH, False),
              (ATT_Q + 2 * ATT_KV + 3 * DN_WIDTH, DN_WIDTH, False),
              (ATT_Q + 2 * ATT_KV + 4 * DN_WIDTH, LANE, False))
SSM_SPLITS = ((0, SSM_D_INNER, False), (SSM_D_INNER, SSM_CONV_DIM, False),
              (SSM_D_INNER + SSM_CONV_DIM, LANE, False))


def _hybrid_layer(x, y_ctx, mp_lat, mp_ctx, norm_g, w_in, w_out, sink, dn_conv, dn_a_log, dn_dt_bias,
                  dn_norm_g, rope_tabs):
    bsz, length, _ = x.shape
    n_pad = HYB_SPLITS[-1][0] + LANE
    w_in_b = _pad_cols(w_in, n_pad).astype(bf16)
    w_out_b = w_out.astype(bf16)
    lat_splits = ((0, ATT_Q, True), (ATT_Q, ATT_KV, True), (ATT_Q + ATT_KV, ATT_KV, False)) + HYB_SPLITS[2:]
    ctx_splits = tuple((s, w, False) for s, w, _ in lat_splits)
    par = jnp.concatenate([_lane_row(dn_a_log, 2 * DN_HEADS), _lane_row(dn_dt_bias, 2 * DN_HEADS)], axis=0)
    s0 = jnp.zeros((bsz, 2, DN_HEADS, DN_HEAD_DIM, DN_HEAD_DIM), f32)

    q_c, k_c, v_c, dqkv_c, z_c, dba_c = _inproj(y_ctx, norm_g, mp_ctx, w_in_b, ctx_splits, 256)
    q_l, k_l, v_l, dqkv_l, z_l, dba_l = _inproj(x, norm_g, mp_lat, w_in_b, lat_splits, 512, rope_tabs)
    kv_c = jnp.concatenate([k_c, v_c], axis=-1)
    kv_l = jnp.concatenate([k_l, v_l], axis=-1)
    att_c = _attention(q_c, kv_c, kv_c, sink, False)
    att_l = _attention(q_l, kv_l, kv_c, sink, True)
    act_c = _dwconv_silu(dqkv_c, dn_conv, 256, 512)
    act_l = _dwconv_silu(dqkv_l, dn_conv, 512, 512)
    of_c, ob_c, s_c = _deltanet(act_c, dba_c, par, s0, True)
    of_l, ob_l, _ = _deltanet(act_l, dba_l, par, s_c, True)
    y_ctx = _hyb_out(att_c, of_c, ob_c, z_c, y_ctx, mp_ctx, dn_norm_g, w_out_b, 256)
    x = _hyb_out(att_l, of_l, ob_l, z_l, x, mp_lat, dn_norm_g, w_out_b, 512)
    return x, y_ctx


def _mamba_layer(x, y_ctx, mp_lat, mp_ctx, norm_g, w_in, conv_w, a_log, dt_bias, d_skip, ssm_norm_g, w_out):
    bsz = x.shape[0]
    n_pad = SSM_SPLITS[-1][0] + LANE
    w_in_b = _pad_cols(w_in, n_pad).astype(bf16)
    par = jnp.concatenate([_lane_row(a_log, 0), _lane_row(dt_bias, 0)], axis=0)
    s0 = jnp.zeros((bsz, 2, SSM_GROUPS, SSM_STATE, SSM_GW), f32)
    xbc_c, dt_c = _inproj(y_ctx, norm_g, mp_ctx, w_in_b, SSM_SPLITS[1:], 256)
    z_l, xbc_l, dt_l = _inproj(x, norm_g, mp_lat, w_in_b, SSM_SPLITS, 256)
    act_c = _dwconv_silu(xbc_c, conv_w, 256, 512)
    act_l = _dwconv_silu(xbc_l, conv_w, 512, 512)
    (s_c,) = _ssd(act_c, dt_c, par, s0, False)
    y_f, y_b, _ = _ssd(act_l, dt_l, par, s_c, True)
    return _ssd_out(y_f, y_b, act_l, z_l, x, mp_lat, d_skip, ssm_norm_g, w_out.astype(bf16), 256)


def kernel(x, c, ctx, c_ctx, mod_w, mod_b, norm_mix_g, norm_ffn_g, ffn_up, ffn_conv, ffn_down,
           hyb_w_in, hyb_w_out, att_sink, dn_conv, dn_a_log, dn_dt_bias, dn_norm_g,
           ssm_w_in, ssm_conv, ssm_a_log, ssm_dt_bias, ssm_d, ssm_norm_g, ssm_w_out, final_g):
    bsz, length, d = x.shape
    depth = mod_w.shape[0]
    rope_tabs = _rope_tables(length)
    cond = jnp.concatenate([c, c_ctx[None, :], jnp.zeros((2 * SUBLANE - bsz - 1, d), f32)], axis=0)
    y_ctx = ctx
    for i in range(depth):
        last = i == depth - 1
        mods = _modulation(cond, mod_w[i], mod_b[i]).reshape(2 * SUBLANE, 6, d)
        mp_lat, mp_ctx = mods[:bsz], mods[bsz:bsz + 1]
        if i % 2 == 0:
            e = i // 2
            x, m_ctx = _hybrid_layer(x, y_ctx, mp_lat, mp_ctx, norm_mix_g[i], hyb_w_in[e], hyb_w_out[e],
                                     att_sink[e], dn_conv[e], dn_a_log[e], dn_dt_bias[e], dn_norm_g[e], rope_tabs)
        else:
            j = i // 2
            x = _mamba_layer(x, y_ctx, mp_lat, mp_ctx, norm_mix_g[i], ssm_w_in[j], ssm_conv[j], ssm_a_log[j],
                             ssm_dt_bias[j], ssm_d[j], ssm_norm_g[j], ssm_w_out[j])
            m_ctx = None
        w_up, w_down = ffn_up[i].astype(bf16), ffn_down[i].astype(bf16)
        x = _conv_ffn(x, norm_ffn_g[i], mp_lat, w_up, ffn_conv[i], w_down, 512,
                      final_g if last else None)
        if not last:
            y_ctx = _conv_ffn(m_ctx, norm_ffn_g[i], mp_ctx, w_up, ffn_conv[i], w_down, 256)
    return x
```

```python
import functools
import math

import numpy as np
import jax
import jax.numpy as jnp
from jax import lax
from jax.experimental import pallas as pl
from jax.experimental.pallas import tpu as pltpu

f32 = jnp.float32
bf16 = jnp.bfloat16

D_MODEL = 1024
GRID_W = 64
EPS = 1e-6
NEG_INF = -1e30

ATT_HEADS = 8
ATT_KV_HEADS = 2
ATT_GROUP = ATT_HEADS // ATT_KV_HEADS
ATT_HEAD_DIM = 64
ATT_BLOCK = 128
ROPE_BASE = 10000.0
ATT_Q = ATT_HEADS * ATT_HEAD_DIM
ATT_KV = ATT_KV_HEADS * ATT_HEAD_DIM

DN_HEADS = 8
DN_HEAD_DIM = 64
DN_WIDTH = DN_HEADS * DN_HEAD_DIM
DN_CONV = 5
DN_CHUNK = 64

SSM_D_INNER = 2 * D_MODEL
SSM_HEAD_DIM = 64
SSM_HEADS = SSM_D_INNER // SSM_HEAD_DIM
SSM_GROUPS = 4
SSM_HPG = SSM_HEADS // SSM_GROUPS
SSM_STATE = 128
SSM_CONV = 5
SSM_CHUNK = 128
SSM_BC = SSM_GROUPS * SSM_STATE
SSM_CONV_DIM = SSM_D_INNER + 2 * SSM_BC
SSM_GW = SSM_HPG * SSM_HEAD_DIM

FFN_DIM = 2816
FFN_CONV = 3
FFN_CHUNK = 256
FFN_ROW_BLOCKS = 2

LANE = 128
SUBLANE = 8
HALO = SUBLANE
FFN_HALO = 2 * SUBLANE
VMEM_CAP_MB = 56

HI = lax.Precision.HIGHEST


def _cparams(sem, vmem_mb):
    return pltpu.CompilerParams(dimension_semantics=sem,
                                vmem_limit_bytes=min(vmem_mb, VMEM_CAP_MB) * 1024 * 1024)


def _bdot(a, b):
    return jnp.dot(a.astype(bf16), b.astype(bf16), preferred_element_type=f32)


def _bdot_nt(a, b):
    return lax.dot_general(a.astype(bf16), b.astype(bf16), (((1,), (1,)), ((), ())),
                           preferred_element_type=f32)


def _bdot_tn(a, b):
    return lax.dot_general(a.astype(bf16), b.astype(bf16), (((0,), (0,)), ((), ())),
                           preferred_element_type=f32)


def _hdot(a, b):
    return jnp.dot(a, b, precision=HI, preferred_element_type=f32)


def _hdot_nt(a, b):
    return lax.dot_general(a, b, (((1,), (1,)), ((), ())), precision=HI, preferred_element_type=f32)


def _split_bf16(a):
    hi = a.astype(bf16)
    lo = (a - hi.astype(f32)).astype(bf16)
    return hi, lo


def _dot3(a, b):
    ah, al = _split_bf16(a)
    bh, bl = _split_bf16(b)
    d = functools.partial(jnp.dot, preferred_element_type=f32)
    return d(ah, bh) + (d(ah, bl) + d(al, bh))


def _silu(x):
    return x * jax.nn.sigmoid(x)


def _softplus(x):
    return jnp.maximum(x, 0.0) + jnp.log1p(jnp.exp(-jnp.abs(x)))


def _norm_mod(x, g, sh, sc):
    ms = jnp.mean(x * x, axis=-1, keepdims=True)
    return (x * lax.rsqrt(ms + EPS) * g) * (1.0 + sc) + sh


def _iota(shape, dim):
    return lax.broadcasted_iota(jnp.int32, shape, dim)


def _mod_kernel(c_ref, w_ref, b_ref, o_ref):
    o_ref[...] = _bdot(_silu(c_ref[...]), w_ref[...]) + b_ref[...]


def _modulation(cond, w, b):
    m, d = cond.shape
    n = w.shape[1]
    tn = 512
    return pl.pallas_call(
        _mod_kernel,
        grid=(n // tn,),
        in_specs=[pl.BlockSpec((m, d), lambda j: (0, 0)),
                  pl.BlockSpec((d, tn), lambda j: (0, j)),
                  pl.BlockSpec((1, tn), lambda j: (0, j))],
        out_specs=pl.BlockSpec((m, tn), lambda j: (0, j)),
        out_shape=jax.ShapeDtypeStruct((m, n), f32),
        compiler_params=_cparams(("arbitrary",), 24),
        name="modulation",
    )(cond, w, b.reshape(1, n))


def _swap_halves(t):
    w = t.shape[-1]
    first = (_iota(t.shape, 1) % ATT_HEAD_DIM) < (ATT_HEAD_DIM // 2)
    return jnp.where(first, pltpu.roll(t, w - ATT_HEAD_DIM // 2, 1), pltpu.roll(t, ATT_HEAD_DIM // 2, 1))


def _inproj_kernel(*refs, splits, rope, sh_row, sc_row):
    if rope:
        x_ref, g_ref, mp_ref, w_ref, cos_ref, sin_ref = refs[:6]
        outs = refs[6:]
    else:
        x_ref, g_ref, mp_ref, w_ref = refs[:4]
        outs = refs[4:]
    mp = mp_ref[0]
    h = _norm_mod(x_ref[0], g_ref[...], mp[sh_row:sh_row + 1], mp[sc_row:sc_row + 1]).astype(bf16)
    for (start, width, roped), o_ref in zip(splits, outs):
        r = jnp.dot(h, w_ref[:, start:start + width], preferred_element_type=f32)
        if roped:
            reps = width // LANE
            cs = jnp.concatenate([cos_ref[...]] * reps, axis=1)
            sn = jnp.concatenate([sin_ref[...]] * reps, axis=1)
            r = r * cs + _swap_halves(r) * sn
        o_ref[0] = r


def _inproj(x, g, mp, w, splits, tm, rope_tabs=None):
    bsz, length, d = x.shape
    n = w.shape[1]
    per_batch = mp.shape[0] != 1
    rope = rope_tabs is not None
    in_specs = [pl.BlockSpec((1, tm, d), lambda b, t: (b, t, 0)),
                pl.BlockSpec((1, d), lambda b, t: (0, 0)),
                pl.BlockSpec((1, 6, d), (lambda b, t: (b, 0, 0)) if per_batch else (lambda b, t: (0, 0, 0))),
                pl.BlockSpec((d, n), lambda b, t: (0, 0))]
    args = [x, g.reshape(1, d), mp, w]
    if rope:
        in_specs += [pl.BlockSpec((tm, LANE), lambda b, t: (t, 0))] * 2
        args += list(rope_tabs)
    out_w = sum(s[1] for s in splits)
    vmem = (2 * tm * d * 4 + 2 * d * n * 2 + 3 * tm * out_w * 4) // (1 << 20) + 8
    return pl.pallas_call(
        functools.partial(_inproj_kernel, splits=tuple(splits), rope=rope, sh_row=0, sc_row=1),
        grid=(bsz, length // tm),
        in_specs=in_specs,
        out_specs=[pl.BlockSpec((1, tm, s[1]), lambda b, t: (b, t, 0)) for s in splits],
        out_shape=[jax.ShapeDtypeStruct((bsz, length, s[1]), f32) for s in splits],
        compiler_params=_cparams(("parallel", "arbitrary"), vmem),
        name="inproj",
    )(*args)


def _dwconv_silu_kernel(xp_ref, x_ref, xn_ref, w_ref, o_ref, ext_ref, *, taps, nt):
    t = pl.program_id(1)
    tm = x_ref.shape[1]
    ext_ref[0:HALO, :] = jnp.where(t > 0, xp_ref[0], 0.0)
    ext_ref[HALO:HALO + tm, :] = x_ref[0]
    ext_ref[HALO + tm:2 * HALO + tm, :] = jnp.where(t < nt - 1, xn_ref[0], 0.0)
    pad = taps // 2
    acc = w_ref[0:1, :] * ext_ref[pl.ds(HALO - pad, tm), :]
    for k in range(1, taps):
        acc = acc + w_ref[k:k + 1, :] * ext_ref[pl.ds(HALO - pad + k, tm), :]
    o_ref[0] = _silu(acc)


def _dwconv_silu(x, w, tm, cw):
    bsz, length, c = x.shape
    taps = w.shape[0]
    nt = length // tm
    hb = tm // HALO
    last_hb = length // HALO - 1
    return pl.pallas_call(
        functools.partial(_dwconv_silu_kernel, taps=taps, nt=nt),
        grid=(bsz, nt, c // cw),
        in_specs=[pl.BlockSpec((1, HALO, cw), lambda b, t, j: (b, jnp.maximum(t * hb - 1, 0), j)),
                  pl.BlockSpec((1, tm, cw), lambda b, t, j: (b, t, j)),
                  pl.BlockSpec((1, HALO, cw), lambda b, t, j: (b, jnp.minimum((t + 1) * hb, last_hb), j)),
                  pl.BlockSpec((taps, cw), lambda b, t, j: (0, j))],
        out_specs=pl.BlockSpec((1, tm, cw), lambda b, t, j: (b, t, j)),
        out_shape=jax.ShapeDtypeStruct((bsz, length, c), f32),
        scratch_shapes=[pltpu.VMEM((tm + 2 * HALO, cw), f32)],
        compiler_params=_cparams(("parallel", "arbitrary", "arbitrary"), 24),
        name="dwconv_silu",
    )(x, x, x, w)


def _attn_kernel(*refs, has_local, nb):
    if has_local:
        sink_ref, q_ref, kp_ref, ko_ref, kn_ref, kc_ref, o_ref = refs
    else:
        sink_ref, q_ref, kc_ref, o_ref = refs
    i = pl.program_id(1)
    scale = ATT_HEAD_DIM ** -0.5
    q = q_ref[0]
    kvc = kc_ref[0]
    hd = ATT_HEAD_DIM
    blk = q.shape[0]
    kv_parts = [kvc]
    if has_local:
        kv_parts += [kp_ref[0], ko_ref[0], kn_ref[0]]
        n_ctx = kvc.shape[0]
        shape = (ATT_GROUP * blk, n_ctx + 3 * ATT_BLOCK)
        qi = _iota(shape, 0) % blk
        col = _iota(shape, 1)
        kp, kn = col - n_ctx, col - (n_ctx + 2 * ATT_BLOCK)
        bad_prev = (kp >= 0) & (kp < ATT_BLOCK) & ((kp < qi) | (i < 1))
        bad_next = (kn >= 0) & ((kn > qi) | (i > nb - 2))
        ok = jnp.logical_not(bad_prev | bad_next)
    scores, vals, sinks = [], [], []
    for kvh in range(ATT_KV_HEADS):
        ks, vs = slice(kvh * hd, (kvh + 1) * hd), slice(ATT_KV + kvh * hd, ATT_KV + (kvh + 1) * hd)
        heads = range(kvh * ATT_GROUP, (kvh + 1) * ATT_GROUP)
        qs = jnp.concatenate([q[:, h * hd:(h + 1) * hd] for h in heads], axis=0)
        keys = jnp.concatenate([part[:, ks] for part in kv_parts], axis=0)
        vals.append(jnp.concatenate([part[:, vs] for part in kv_parts], axis=0))
        sinks.append(jnp.concatenate([jnp.full((blk, 1), sink_ref[h], f32) for h in heads], axis=0))
        s = _bdot_nt(qs, keys) * scale
        scores.append(jnp.where(ok, s, NEG_INF) if has_local else s)
    for kvh in range(ATT_KV_HEADS):
        s, sink = scores[kvh], sinks[kvh]
        m = jnp.maximum(sink, jnp.max(s, axis=-1, keepdims=True))
        p = jnp.exp(s - m)
        denom = jnp.exp(sink - m) + jnp.sum(p, axis=-1, keepdims=True)
        o = _bdot(p, vals[kvh]) / denom
        for gi in range(ATT_GROUP):
            h = kvh * ATT_GROUP + gi
            o_ref[0, :, h * hd:(h + 1) * hd] = o[gi * blk:(gi + 1) * blk]


def _attention(q, kv, kv_ctx, sink, has_local):
    bsz, length, _ = q.shape
    ctx_len = kv_ctx.shape[1]
    nb = length // ATT_BLOCK
    kvw = 2 * ATT_KV
    in_specs = [pl.BlockSpec(memory_space=pltpu.SMEM),
                pl.BlockSpec((1, ATT_BLOCK, ATT_Q), lambda b, i: (b, i, 0))]
    args = [sink, q]
    if has_local:
        in_specs += [pl.BlockSpec((1, ATT_BLOCK, kvw), lambda b, i: (b, jnp.maximum(i - 1, 0), 0)),
                     pl.BlockSpec((1, ATT_BLOCK, kvw), lambda b, i: (b, i, 0)),
                     pl.BlockSpec((1, ATT_BLOCK, kvw), lambda b, i: (b, jnp.minimum(i + 1, nb - 1), 0))]
        args += [kv, kv, kv]
    in_specs.append(pl.BlockSpec((1, ctx_len, kvw), lambda b, i: (b, 0, 0)))
    args.append(kv_ctx)
    return pl.pallas_call(
        functools.partial(_attn_kernel, has_local=has_local, nb=nb),
        grid=(bsz, nb),
        in_specs=in_specs,
        out_specs=pl.BlockSpec((1, ATT_BLOCK, ATT_Q), lambda b, i: (b, i, 0)),
        out_shape=jax.ShapeDtypeStruct((bsz, length, ATT_Q), f32),
        compiler_params=_cparams(("parallel", "arbitrary"), 24),
        name="attention",
    )(*args)


def _blockdiag(x):
    left = _iota(x.shape, 1) < DN_HEAD_DIM
    zero = jnp.zeros_like(x)
    return jnp.concatenate([jnp.where(left, x, zero), jnp.where(left, zero, x)], axis=0)


def _dot3_bd(lhs, *xs):
    lh, ll = _split_bf16(lhs)
    his, los = zip(*[_split_bf16(x) for x in xs])
    bh = jnp.concatenate([_blockdiag(h) for h in his], axis=1)
    bl = jnp.concatenate([_blockdiag(l) for l in los], axis=1)
    d = functools.partial(jnp.dot, preferred_element_type=f32)
    return d(lh, bh) + (d(lh, bl) + d(ll, bh))


def _dn_step(qkvs, dbas, par_ref, s_ref, o_refs, with_out):
    c = DN_CHUNK
    hd = DN_HEAD_DIM
    row = _iota((c, LANE), 0)
    lane = _iota((c, LANE), 1)
    left = lane < hd
    colh = lane % hd
    incl = (row >= colh, row <= colh)
    strict = (row > colh, row < colh)
    last = (c - 1, 0)
    tri = [(_iota((c, c), 0) >= _iota((c, c), 1)).astype(f32), (_iota((c, c), 0) <= _iota((c, c), 1)).astype(f32)]
    same_head = (_iota((LANE, LANE), 0) < hd) == (_iota((LANE, LANE), 1) < hd)
    probs = [(d, j) for d in range(2) for j in range(DN_HEADS // 2)]

    sig, g_all, g_all_t = [], [], []
    for d in range(2):
        sig.append(jax.nn.sigmoid(dbas[d]))
        ld = -jnp.exp(par_ref[0:1, :]) * _softplus(dbas[d] + par_ref[1:2, :])
        g_all.append(_hdot(tri[d], ld))
        g_all_t.append(lax.dot_general(ld, tri[d], (((0,), (1,)), ((), ())), precision=HI,
                                       preferred_element_type=f32))

    def l2n(x):
        x2 = x * x
        ssa = jnp.sum(jnp.where(left, x2, 0.0), axis=-1, keepdims=True)
        ssb = jnp.sum(jnp.where(left, 0.0, x2), axis=-1, keepdims=True)
        return x * lax.rsqrt(jnp.where(left, ssa, ssb) + EPS)

    bt, eg, e_incl, kn, qn, rv, rk, kd, ge = {}, {}, {}, {}, {}, {}, {}, {}, {}
    for p in probs:
        d, j = p
        ha, hb = 2 * j, 2 * j + 1
        pair = lambda arr, base: jnp.where(left, arr[:, base + ha:base + ha + 1], arr[:, base + hb:base + hb + 1])
        g = pair(g_all[d], 2 * DN_HEADS + d * DN_HEADS)
        g_row = jnp.concatenate([g_all_t[d][2 * DN_HEADS + d * DN_HEADS + ha:2 * DN_HEADS + d * DN_HEADS + ha + 1, :],
                                 g_all_t[d][2 * DN_HEADS + d * DN_HEADS + hb:2 * DN_HEADS + d * DN_HEADS + hb + 1, :]],
                                axis=1)
        g_last = g[last[d]:last[d] + 1, :]
        kp = qkvs[d][:, DN_WIDTH + j * LANE:DN_WIDTH + (j + 1) * LANE]
        vp = qkvs[d][:, 2 * DN_WIDTH + j * LANE:2 * DN_WIDTH + (j + 1) * LANE]
        bt[p] = pair(sig[d], d * DN_HEADS)
        eg[p] = jnp.exp(g)
        e_incl[p] = jnp.exp(jnp.where(incl[d], g - g_row, -jnp.inf))
        kn[p] = l2n(kp)
        rv[p] = bt[p] * vp
        rk[p] = (bt[p] * eg[p]) * kn[p]
        kd[p] = kn[p] * jnp.exp(g_last - g)
        ge[p] = jnp.exp(g_last)
        if with_out:
            qn[p] = l2n(qkvs[d][:, j * LANE:(j + 1) * LANE]) * (hd ** -0.5)

    if with_out:
        qk = {p: _bdot_nt(jnp.concatenate([kn[p], qn[p]], axis=0), _blockdiag(kn[p].astype(bf16))) for p in probs}
    else:
        qk = {p: _bdot_nt(kn[p], _blockdiag(kn[p].astype(bf16))) for p in probs}
    a = {p: bt[p] * qk[p][:c] * jnp.where(strict[p[0]], e_incl[p], 0.0) for p in probs}

    nm = {p: -a[p] for p in probs}
    ak = {p: _dot3_bd(a[p], a[p]) for p in probs}
    for _ in range(4):
        r = {p: _dot3_bd(jnp.concatenate([nm[p], ak[p]], axis=0), ak[p]) for p in probs}
        nm = {p: nm[p] + ak[p] + r[p][:c] for p in probs}
        ak = {p: r[p][c:] for p in probs}
    r = {p: _dot3_bd(nm[p], ak[p]) for p in probs}
    nm = {p: nm[p] + ak[p] + r[p] for p in probs}

    uw = {p: _dot3_bd(nm[p], rv[p], rk[p]) for p in probs}
    s = {p: s_ref[0, p[0], p[1]] for p in probs}
    u = {p: rv[p] + uw[p][:, :LANE] - _bdot(rk[p] + uw[p][:, LANE:], s[p]) for p in probs}
    if with_out:
        for p in probs:
            d, j = p
            lhs = jnp.concatenate([qn[p] * eg[p], qk[p][c:] * e_incl[p]], axis=1)
            rhs = jnp.concatenate([s[p].astype(bf16), _blockdiag(u[p].astype(bf16))], axis=0)
            o_refs[d][0, :, j * LANE:(j + 1) * LANE] = _bdot(lhs, rhs)
    for p in probs:
        s_ref[0, p[0], p[1]] = ge[p] * s[p] + jnp.where(same_head, _bdot_tn(kd[p], u[p]), 0.0)


def _dn_kernel(*refs, with_out):
    if with_out:
        qf_ref, qb_ref, bf_ref, bb_ref, par_ref, s0_ref, of_ref, ob_ref, s_ref = refs
    else:
        qf_ref, qb_ref, bf_ref, bb_ref, par_ref, s0_ref, s_ref = refs
        of_ref = ob_ref = None

    @pl.when(pl.program_id(1) == 0)
    def _():
        s_ref[...] = s0_ref[...]

    _dn_step((qf_ref[0], qb_ref[0]), (bf_ref[0], bb_ref[0]), par_ref, s_ref, (of_ref, ob_ref), with_out)


def _dn_state_shape(bsz):
    return (bsz, 2, DN_HEADS // 2, 2 * DN_HEAD_DIM, 2 * DN_HEAD_DIM)


def _deltanet(qkv, dba, par, s0, with_out):
    bsz, length, qw = qkv.shape
    nc = length // DN_CHUNK
    c = DN_CHUNK
    st_shape = _dn_state_shape(bsz)
    st_spec = pl.BlockSpec((1,) + st_shape[1:], lambda b, s: (b, 0, 0, 0, 0))
    in_specs = [pl.BlockSpec((1, c, qw), lambda b, s: (b, s, 0)),
                pl.BlockSpec((1, c, qw), lambda b, s: (b, nc - 1 - s, 0)),
                pl.BlockSpec((1, c, LANE), lambda b, s: (b, s, 0)),
                pl.BlockSpec((1, c, LANE), lambda b, s: (b, nc - 1 - s, 0)),
                pl.BlockSpec((2, LANE), lambda b, s: (0, 0)),
                st_spec]
    out_specs, out_shape = [], []
    if with_out:
        out_specs += [pl.BlockSpec((1, c, DN_WIDTH), lambda b, s: (b, s, 0)),
                      pl.BlockSpec((1, c, DN_WIDTH), lambda b, s: (b, nc - 1 - s, 0))]
        out_shape += [jax.ShapeDtypeStruct((bsz, length, DN_WIDTH), f32)] * 2
    out_specs.append(st_spec)
    out_shape.append(jax.ShapeDtypeStruct(st_shape, f32))
    return pl.pallas_call(
        functools.partial(_dn_kernel, with_out=with_out),
        grid=(bsz, nc),
        in_specs=in_specs,
        out_specs=out_specs,
        out_shape=out_shape,
        compiler_params=_cparams(("parallel", "arbitrary"), 24),
        name="deltanet",
    )(qkv, qkv, dba, dba, par, s0)


def _hyb_out_kernel(att_ref, of_ref, ob_ref, z_ref, x_ref, mp_ref, g_ref, bd_ref, w_ref, o_ref):
    dn = of_ref[0] + ob_ref[0]
    ms = _hdot(dn * dn, bd_ref[...])
    gated = (dn * lax.rsqrt(ms + EPS) * g_ref[...]) * _silu(z_ref[0])
    m = _bdot(att_ref[0], w_ref[0:ATT_Q, :]) + _bdot(gated, w_ref[ATT_Q:ATT_Q + DN_WIDTH, :])
    o_ref[0] = x_ref[0] + mp_ref[0][2:3] * m


def _hyb_out(att, o_f, o_b, z, x, mp, dn_norm_g, w_out, tm):
    bsz, length, d = x.shape
    per_batch = mp.shape[0] != 1
    head_of = np.arange(DN_WIDTH) // DN_HEAD_DIM
    bd = jnp.asarray((head_of[:, None] == head_of[None, :]).astype(np.float32) / DN_HEAD_DIM)
    g_row = jnp.tile(dn_norm_g, DN_HEADS).reshape(1, DN_WIDTH)
    tok = lambda w: pl.BlockSpec((1, tm, w), lambda b, t: (b, t, 0))
    const = lambda shape: pl.BlockSpec(shape, lambda b, t: (0,) * len(shape))
    return pl.pallas_call(
        _hyb_out_kernel,
        grid=(bsz, length // tm),
        in_specs=[tok(ATT_Q), tok(DN_WIDTH), tok(DN_WIDTH), tok(DN_WIDTH), tok(d),
                  pl.BlockSpec((1, 6, d), (lambda b, t: (b, 0, 0)) if per_batch else (lambda b, t: (0, 0, 0))),
                  const((1, DN_WIDTH)), const((DN_WIDTH, DN_WIDTH)), const((ATT_Q + DN_WIDTH, d))],
        out_specs=tok(d),
        out_shape=jax.ShapeDtypeStruct((bsz, length, d), f32),
        compiler_params=_cparams(("parallel", "arbitrary"), 40),
        name="hyb_out",
    )(att, o_f, o_b, z, x, mp, g_row, bd, w_out)


def _ssd_out_kernel(yf_ref, yb_ref, xa_ref, xb_ref, z_ref, x_ref, mp_ref, dsk_ref, g_ref, w_ref, o_ref):
    xs = jnp.concatenate([xa_ref[0], xb_ref[0]], axis=1)
    y = (yf_ref[0] + yb_ref[0] + dsk_ref[...] * xs) * _silu(z_ref[0])
    gw = SSM_D_INNER // SSM_GROUPS
    m = None
    for gi in range(SSM_GROUPS):
        yg = y[:, gi * gw:(gi + 1) * gw]
        ms = jnp.mean(yg * yg, axis=-1, keepdims=True)
        part = _bdot(yg * lax.rsqrt(ms + EPS) * g_ref[:, gi * gw:(gi + 1) * gw], w_ref[gi * gw:(gi + 1) * gw, :])
        m = part if m is None else m + part
    o_ref[0] = x_ref[0] + mp_ref[0][2:3] * m


def _ssd_out(y_f, y_b, xbc_act, z, x, mp, d_skip, norm_g, w_out, tm):
    bsz, length, d = x.shape
    di = SSM_D_INNER
    half = di // 2
    tok = lambda w: pl.BlockSpec((1, tm, w), lambda b, t: (b, t, 0))
    const = lambda shape: pl.BlockSpec(shape, lambda b, t: (0,) * len(shape))
    return pl.pallas_call(
        _ssd_out_kernel,
        grid=(bsz, length // tm),
        in_specs=[tok(di), tok(di),
                  pl.BlockSpec((1, tm, half), lambda b, t: (b, t, 0)),
                  pl.BlockSpec((1, tm, half), lambda b, t: (b, t, 1)),
                  tok(di), tok(d),
                  pl.BlockSpec((1, 6, d), lambda b, t: (b, 0, 0)),
                  const((1, di)), const((1, di)), const((di, d))],
        out_specs=tok(d),
        out_shape=jax.ShapeDtypeStruct((bsz, length, d), f32),
        compiler_params=_cparams(("parallel", "arbitrary"), 48),
        name="ssd_out",
    )(y_f, y_b, xbc_act, xbc_act, z, x, mp,
      jnp.repeat(d_skip, SSM_HEAD_DIM).reshape(1, di), norm_g.reshape(1, di), w_out)


def _ffn_kernel(xp_ref, x_ref, xn_ref, g_ref, mp_ref, wu_ref, cw_ref, wd_ref, fg_ref,
                o_ref, h_ref, u_ref, a_ref, *, nt, nj, final_norm):
    t = pl.program_id(1)
    tm = x_ref.shape[1]
    fc = FFN_CHUNK
    hl = FFN_HALO
    pad = FFN_CONV // 2
    mp = mp_ref[0]
    nm = lambda v: _norm_mod(v, g_ref[...], mp[3:4], mp[4:5])
    h_ref[0:hl, :] = jnp.where(t > 0, nm(xp_ref[0]), 0.0).astype(bf16)
    h_ref[hl:hl + tm, :] = nm(x_ref[0]).astype(bf16)
    h_ref[hl + tm:2 * hl + tm, :] = jnp.where(t < nt - 1, nm(xn_ref[0]), 0.0).astype(bf16)

    def up(c):
        cols = slice(c * 2 * fc, (c + 1) * 2 * fc)
        u_ref[:, cols] = jnp.dot(h_ref[...], wu_ref[:, cols], preferred_element_type=f32)

    def conv_gate(c):
        vc, gc = c * 2 * fc, c * 2 * fc + fc
        val = cw_ref[0:1, vc:vc + fc] * u_ref[pl.ds(hl - pad, tm), vc:vc + fc]
        gate = cw_ref[0:1, gc:gc + fc] * u_ref[pl.ds(hl - pad, tm), gc:gc + fc]
        for k in range(1, FFN_CONV):
            val = val + cw_ref[k:k + 1, vc:vc + fc] * u_ref[pl.ds(hl - pad + k, tm), vc:vc + fc]
            gate = gate + cw_ref[k:k + 1, gc:gc + fc] * u_ref[pl.ds(hl - pad + k, tm), gc:gc + fc]
        a_ref[:, c * fc:(c + 1) * fc] = (_silu(gate) * val).astype(bf16)

    up(0)
    for c in range(nj):
        if c + 1 < nj:
            up(c + 1)
        conv_gate(c)
    y = x_ref[0] + mp[5:6] * jnp.dot(a_ref[...], wd_ref[...], preferred_element_type=f32)
    if final_norm:
        y = y * lax.rsqrt(jnp.mean(y * y, axis=-1, keepdims=True) + EPS) * fg_ref[...]
    o_ref[0] = y


def _ffn_chunk_major(w):
    r = w.shape[0]
    nj = FFN_DIM // FFN_CHUNK
    return w.reshape(r, 2, nj, FFN_CHUNK).transpose(0, 2, 1, 3).reshape(r, 2 * FFN_DIM)


def _conv_ffn(x, g, mp, w_up, w_conv, w_down, tm, final_g=None):
    bsz, length, d = x.shape
    per_batch = mp.shape[0] != 1
    nt = length // tm
    hl = FFN_HALO
    hb = tm // hl
    last_hb = length // hl - 1
    nj = FFN_DIM // FFN_CHUNK
    fc = FFN_CHUNK
    final_norm = final_g is not None
    fg = (final_g if final_norm else g).reshape(1, d)
    ff = FFN_DIM
    resident = lambda shape: pl.BlockSpec(shape, lambda b, t: (0,) * len(shape), pipeline_mode=pl.Buffered(1))
    vmem = (4 * tm * d * 4 + (tm + 2 * hl) * (d * 2 + 2 * ff * 4) + tm * ff * 2 + 3 * d * ff * 2) // (1 << 20) + 8
    return pl.pallas_call(
        functools.partial(_ffn_kernel, nt=nt, nj=nj, final_norm=final_norm),
        grid=(bsz, nt),
        in_specs=[pl.BlockSpec((1, hl, d), lambda b, t: (b, jnp.maximum(t * hb - 1, 0), 0)),
                  pl.BlockSpec((1, tm, d), lambda b, t: (b, t, 0)),
                  pl.BlockSpec((1, hl, d), lambda b, t: (b, jnp.minimum((t + 1) * hb, last_hb), 0)),
                  pl.BlockSpec((1, d), lambda b, t: (0, 0)),
                  pl.BlockSpec((1, 6, d), (lambda b, t: (b, 0, 0)) if per_batch else (lambda b, t: (0, 0, 0))),
                  resident((d, 2 * ff)),
                  resident((FFN_CONV, 2 * ff)),
                  resident((ff, d)),
                  pl.BlockSpec((1, d), lambda b, t: (0, 0))],
        out_specs=pl.BlockSpec((1, tm, d), lambda b, t: (b, t, 0)),
        out_shape=jax.ShapeDtypeStruct((bsz, length, d), f32),
        scratch_shapes=[pltpu.VMEM((tm + 2 * hl, d), bf16),
                        pltpu.VMEM((tm + 2 * hl, 2 * ff), f32),
                        pltpu.VMEM((tm, ff), bf16)],
        compiler_params=_cparams(("parallel", "arbitrary"), vmem),
        name="conv_ffn",
    )(x, x, x, g.reshape(1, d), mp, w_up, w_conv, w_down, fg)


def _ssd_step(xbcs, dtrs, par_ref, s_ref, y_refs, with_y):
    c = SSM_CHUNK
    p = SSM_HEAD_DIM
    n = SSM_STATE
    row = _iota((c, c), 0)
    col = _iota((c, c), 1)
    incl = (row >= col, row <= col)
    last = (c - 1, 0)
    a_row = -jnp.exp(par_ref[0:1, :])
    g, g_t, dt_t, wdt_t, ge = [], [], [], [], []
    for d in range(2):
        dt = _softplus(dtrs[d] + par_ref[1:2, :])
        da = dt * a_row
        tri = incl[d].astype(f32)
        g.append(_hdot(tri, da))
        g_t.append(lax.dot_general(da, tri, (((0,), (1,)), ((), ())), precision=HI,
                                   preferred_element_type=f32))
        dt_t.append(dt.T)
        wdt_t.append(jnp.exp(g_t[d][:, last[d]:last[d] + 1] - g_t[d]) * dt_t[d])
        ge.append(jnp.exp(g[d][last[d]:last[d] + 1, :]))

    groups = [(d, gi) for d in range(2) for gi in range(SSM_GROUPS)]
    s_in, b_t, scores, y_inter = {}, {}, {}, {}
    for d, gi in groups:
        bg = xbcs[d][:, SSM_D_INNER + gi * n:SSM_D_INNER + (gi + 1) * n]
        s_in[d, gi] = s_ref[0, d, gi]
        b_t[d, gi] = bg.T
        if with_y:
            cg = xbcs[d][:, SSM_D_INNER + SSM_BC + gi * n:SSM_D_INNER + SSM_BC + (gi + 1) * n]
            scores[d, gi] = _bdot_nt(cg, bg)
            y_inter[d, gi] = _bdot(cg, s_in[d, gi])

    left = _iota((c, LANE), 1) < p
    left_row = _iota((1, LANE), 1) < p
    for d, gi in groups:
        for pj in range(SSM_HPG // 2):
            h0 = gi * SSM_HPG + 2 * pj
            cls = (d * SSM_HEADS + h0, d * SSM_HEADS + h0 + 1)
            ps = slice(pj * LANE, (pj + 1) * LANE)
            xs_bd = _blockdiag(xbcs[d][:, h0 * p:h0 * p + LANE].astype(bf16))
            bw = jnp.concatenate([b_t[d, gi] * wdt_t[d][cl:cl + 1, :] for cl in cls], axis=1)
            if with_y:
                gcs = [jnp.broadcast_to(g[d][:, cl:cl + 1], (c, c)) for cl in cls]
                m = jnp.concatenate(
                    [scores[d, gi] * jnp.exp(jnp.where(incl[d], gc - g_t[d][cl:cl + 1, :], -jnp.inf))
                     * dt_t[d][cl:cl + 1, :] for gc, cl in zip(gcs, cls)], axis=1)
                r = _bdot(jnp.concatenate([m, bw], axis=0), xs_bd)
                y_refs[d][0, :, h0 * p:h0 * p + LANE] = (
                    y_inter[d, gi][:, ps] * jnp.exp(jnp.where(left, gcs[0], gcs[1])) + r[:c])
                ds = r[c:]
            else:
                ds = _bdot(bw, xs_bd)
            ge_row = jnp.where(left_row, ge[d][:, cls[0]:cls[0] + 1], ge[d][:, cls[1]:cls[1] + 1])
            s_ref[0, d, gi, :, ps] = ge_row * s_in[d, gi][:, ps] + ds


def _ssd_kernel(*refs, with_y):
    if with_y:
        xf_ref, xb_ref, df_ref, db_ref, par_ref, s0_ref, yf_ref, yb_ref, s_ref = refs
    else:
        xf_ref, xb_ref, df_ref, db_ref, par_ref, s0_ref, s_ref = refs
        yf_ref = yb_ref = None

    @pl.when(pl.program_id(1) == 0)
    def _():
        s_ref[...] = s0_ref[...]

    _ssd_step((xf_ref[0], xb_ref[0]), (df_ref[0], db_ref[0]), par_ref, s_ref, (yf_ref, yb_ref), with_y)


def _ssd(xbc, dtr, par, s0, with_y):
    bsz, length, xw = xbc.shape
    c = SSM_CHUNK
    nc = length // c
    st_shape = (bsz, 2, SSM_GROUPS, SSM_STATE, SSM_GW)
    st_spec = pl.BlockSpec((1,) + st_shape[1:], lambda b, s: (b, 0, 0, 0, 0))
    in_specs = [pl.BlockSpec((1, c, xw), lambda b, s: (b, s, 0)),
                pl.BlockSpec((1, c, xw), lambda b, s: (b, nc - 1 - s, 0)),
                pl.BlockSpec((1, c, LANE), lambda b, s: (b, s, 0)),
                pl.BlockSpec((1, c, LANE), lambda b, s: (b, nc - 1 - s, 0)),
                pl.BlockSpec((2, LANE), lambda b, s: (0, 0)),
                st_spec]
    out_specs, out_shape = [], []
    if with_y:
        out_specs += [pl.BlockSpec((1, c, SSM_D_INNER), lambda b, s: (b, s, 0)),
                      pl.BlockSpec((1, c, SSM_D_INNER), lambda b, s: (b, nc - 1 - s, 0))]
        out_shape += [jax.ShapeDtypeStruct((bsz, length, SSM_D_INNER), f32)] * 2
    out_specs.append(st_spec)
    out_shape.append(jax.ShapeDtypeStruct(st_shape, f32))
    return pl.pallas_call(
        functools.partial(_ssd_kernel, with_y=with_y),
        grid=(bsz, nc),
        in_specs=in_specs,
        out_specs=out_specs,
        out_shape=out_shape,
        compiler_params=_cparams(("parallel", "arbitrary"), 40),
        name="ssd_scan",
    )(xbc, xbc, dtr, dtr, par, s0)


def _rope_tables(length):
    rows = length // GRID_W
    row = jnp.repeat(jnp.arange(rows, dtype=f32), GRID_W)
    col = jnp.tile(jnp.arange(GRID_W, dtype=f32), rows)
    n_freq = ATT_HEAD_DIM // 4
    inv = ROPE_BASE ** (-jnp.arange(n_freq, dtype=f32) / n_freq)
    ang = jnp.concatenate([row[:, None] * inv, col[:, None] * inv], axis=-1)
    cos, sin = jnp.cos(ang), jnp.sin(ang)
    reps = LANE // ATT_HEAD_DIM
    return (jnp.tile(jnp.concatenate([cos, cos], axis=-1), (1, reps)),
            jnp.tile(jnp.concatenate([-sin, sin], axis=-1), (1, reps)))


def _pad_cols(w, n):
    return jnp.pad(w, ((0, 0), (0, n - w.shape[1])))


def _lane_row(v, offset):
    return jnp.pad(v.reshape(1, -1), ((0, 0), (offset, LANE - offset - v.size)))


HYB_SPLITS = ((0, ATT_Q, True), (ATT_Q, 2 * ATT_KV, True), (ATT_Q + 2 * ATT_KV, 3 * DN_WIDTH, False),
              (ATT_Q + 2 * ATT_KV + 3 * DN_WIDTH, DN_WIDTH, False),
              (ATT_Q + 2 * ATT_KV + 4 * DN_WIDTH, LANE, False))
SSM_SPLITS = ((0, SSM_D_INNER, False), (SSM_D_INNER, SSM_CONV_DIM, False),
              (SSM_D_INNER + SSM_CONV_DIM, LANE, False))


def _hybrid_layer(x, y_ctx, mp_lat, mp_ctx, norm_g, w_in, w_out, sink, dn_conv, dn_a_log, dn_dt_bias,
                  dn_norm_g, rope_tabs):
    bsz, length, _ = x.shape
    n_pad = HYB_SPLITS[-1][0] + LANE
    w_in_b = _pad_cols(w_in, n_pad).astype(bf16)
    w_out_b = w_out.astype(bf16)
    lat_splits = ((0, ATT_Q, True), (ATT_Q, ATT_KV, True), (ATT_Q + ATT_KV, ATT_KV, False)) + HYB_SPLITS[2:]
    ctx_splits = tuple((s, w, False) for s, w, _ in lat_splits)
    par = jnp.concatenate([_lane_row(dn_a_log, 2 * DN_HEADS), _lane_row(dn_dt_bias, 2 * DN_HEADS)], axis=0)
    s0 = jnp.zeros(_dn_state_shape(bsz), f32)

    q_c, k_c, v_c, dqkv_c, z_c, dba_c = _inproj(y_ctx, norm_g, mp_ctx, w_in_b, ctx_splits, 256)
    q_l, k_l, v_l, dqkv_l, z_l, dba_l = _inproj(x, norm_g, mp_lat, w_in_b, lat_splits, 512, rope_tabs)
    kv_c = jnp.concatenate([k_c, v_c], axis=-1)
    kv_l = jnp.concatenate([k_l, v_l], axis=-1)
    att_c = _attention(q_c, kv_c, kv_c, sink, False)
    att_l = _attention(q_l, kv_l, kv_c, sink, True)
    act_c = _dwconv_silu(dqkv_c, dn_conv, 256, 512)
    act_l = _dwconv_silu(dqkv_l, dn_conv, 512, 512)
    of_c, ob_c, s_c = _deltanet(act_c, dba_c, par, s0, True)
    of_l, ob_l, _ = _deltanet(act_l, dba_l, par, s_c, True)
    y_ctx = _hyb_out(att_c, of_c, ob_c, z_c, y_ctx, mp_ctx, dn_norm_g, w_out_b, 256)
    x = _hyb_out(att_l, of_l, ob_l, z_l, x, mp_lat, dn_norm_g, w_out_b, 512)
    return x, y_ctx


def _mamba_layer(x, y_ctx, mp_lat, mp_ctx, norm_g, w_in, conv_w, a_log, dt_bias, d_skip, ssm_norm_g, w_out):
    bsz = x.shape[0]
    n_pad = SSM_SPLITS[-1][0] + LANE
    w_in_b = _pad_cols(w_in, n_pad).astype(bf16)
    par = jnp.concatenate([_lane_row(a_log, 0), _lane_row(dt_bias, 0)], axis=0)
    s0 = jnp.zeros((bsz, 2, SSM_GROUPS, SSM_STATE, SSM_GW), f32)
    xbc_c, dt_c = _inproj(y_ctx, norm_g, mp_ctx, w_in_b, SSM_SPLITS[1:], 256)
    z_l, xbc_l, dt_l = _inproj(x, norm_g, mp_lat, w_in_b, SSM_SPLITS, 256)
    act_c = _dwconv_silu(xbc_c, conv_w, 256, 512)
    act_l = _dwconv_silu(xbc_l, conv_w, 512, 512)
    (s_c,) = _ssd(act_c, dt_c, par, s0, False)
    y_f, y_b, _ = _ssd(act_l, dt_l, par, s_c, True)
    return _ssd_out(y_f, y_b, act_l, z_l, x, mp_lat, d_skip, ssm_norm_g, w_out.astype(bf16), 256)


def kernel(x, c, ctx, c_ctx, mod_w, mod_b, norm_mix_g, norm_ffn_g, ffn_up, ffn_conv, ffn_down,
           hyb_w_in, hyb_w_out, att_sink, dn_conv, dn_a_log, dn_dt_bias, dn_norm_g,
           ssm_w_in, ssm_conv, ssm_a_log, ssm_dt_bias, ssm_d, ssm_norm_g, ssm_w_out, final_g):
    bsz, length, d = x.shape
    depth = mod_w.shape[0]
    rope_tabs = _rope_tables(length)
    cond = jnp.concatenate([c, c_ctx[None, :], jnp.zeros((2 * SUBLANE - bsz - 1, d), f32)], axis=0)
    y_ctx = ctx
    for i in range(depth):
        last = i == depth - 1
        mods = _modulation(cond, mod_w[i], mod_b[i]).reshape(2 * SUBLANE, 6, d)
        mp_lat, mp_ctx = mods[:bsz], mods[bsz:bsz + 1]
        if i % 2 == 0:
            e = i // 2
            x, m_ctx = _hybrid_layer(x, y_ctx, mp_lat, mp_ctx, norm_mix_g[i], hyb_w_in[e], hyb_w_out[e],
                                     att_sink[e], dn_conv[e], dn_a_log[e], dn_dt_bias[e], dn_norm_g[e], rope_tabs)
        else:
            j = i // 2
            x = _mamba_layer(x, y_ctx, mp_lat, mp_ctx, norm_mix_g[i], ssm_w_in[j], ssm_conv[j], ssm_a_log[j],
                             ssm_dt_bias[j], ssm_d[j], ssm_norm_g[j], ssm_w_out[j])
            m_ctx = None
        w_up, w_down = _ffn_chunk_major(ffn_up[i]).astype(bf16), ffn_down[i].astype(bf16)
        w_conv = _ffn_chunk_major(ffn_conv[i])
        x = _conv_ffn(x, norm_ffn_g[i], mp_lat, w_up, w_conv, w_down, 512, final_g if last else None)
        if not last:
            y_ctx = _conv_ffn(m_ctx, norm_ffn_g[i], mp_ctx, w_up, w_conv, w_down, 256)
    return x
```

```python
import functools
import math

import numpy as np
import jax
import jax.numpy as jnp
from jax import lax
from jax.experimental import pallas as pl
from jax.experimental.pallas import tpu as pltpu

f32 = jnp.float32
bf16 = jnp.bfloat16

D_MODEL = 1024
GRID_W = 64
EPS = 1e-6
NEG_INF = -1e30

ATT_HEADS = 8
ATT_KV_HEADS = 2
ATT_GROUP = ATT_HEADS // ATT_KV_HEADS
ATT_HEAD_DIM = 64
ATT_BLOCK = 128
ROPE_BASE = 10000.0
ATT_Q = ATT_HEADS * ATT_HEAD_DIM
ATT_KV = ATT_KV_HEADS * ATT_HEAD_DIM

DN_HEADS = 8
DN_HEAD_DIM = 64
DN_WIDTH = DN_HEADS * DN_HEAD_DIM
DN_CONV = 5
DN_CHUNK = 64

SSM_D_INNER = 2 * D_MODEL
SSM_HEAD_DIM = 64
SSM_HEADS = SSM_D_INNER // SSM_HEAD_DIM
SSM_GROUPS = 4
SSM_HPG = SSM_HEADS // SSM_GROUPS
SSM_STATE = 128
SSM_CONV = 5
SSM_CHUNK = 128
SSM_BC = SSM_GROUPS * SSM_STATE
SSM_CONV_DIM = SSM_D_INNER + 2 * SSM_BC
SSM_GW = SSM_HPG * SSM_HEAD_DIM

FFN_DIM = 2816
FFN_CONV = 3
FFN_CHUNK = 256
FFN_ROW_BLOCKS = 2

LANE = 128
SUBLANE = 8
FFN_HALO = 2 * SUBLANE
VMEM_CAP_MB = 56

HI = lax.Precision.HIGHEST


def _cparams(sem, vmem_mb):
    return pltpu.CompilerParams(dimension_semantics=sem,
                                vmem_limit_bytes=min(vmem_mb, VMEM_CAP_MB) * 1024 * 1024)


def _bdot(a, b):
    return jnp.dot(a.astype(bf16), b.astype(bf16), preferred_element_type=f32)


def _bdot_nt(a, b):
    return lax.dot_general(a.astype(bf16), b.astype(bf16), (((1,), (1,)), ((), ())),
                           preferred_element_type=f32)


def _bdot_tn(a, b):
    return lax.dot_general(a.astype(bf16), b.astype(bf16), (((0,), (0,)), ((), ())),
                           preferred_element_type=f32)


def _hdot(a, b):
    return jnp.dot(a, b, precision=HI, preferred_element_type=f32)


def _hdot_nt(a, b):
    return lax.dot_general(a, b, (((1,), (1,)), ((), ())), precision=HI, preferred_element_type=f32)


def _split_bf16(a):
    hi = a.astype(bf16)
    lo = (a - hi.astype(f32)).astype(bf16)
    return hi, lo


def _dot3(a, b):
    ah, al = _split_bf16(a)
    bh, bl = _split_bf16(b)
    d = functools.partial(jnp.dot, preferred_element_type=f32)
    return d(ah, bh) + (d(ah, bl) + d(al, bh))


def _silu(x):
    return x * jax.nn.sigmoid(x)


def _softplus(x):
    return jnp.maximum(x, 0.0) + jnp.log1p(jnp.exp(-jnp.abs(x)))


def _norm_mod(x, g, sh, sc):
    ms = jnp.mean(x * x, axis=-1, keepdims=True)
    return (x * lax.rsqrt(ms + EPS) * g) * (1.0 + sc) + sh


def _iota(shape, dim):
    return lax.broadcasted_iota(jnp.int32, shape, dim)


def _mod_kernel(c_ref, w_ref, b_ref, o_ref):
    o_ref[...] = _bdot(_silu(c_ref[...]), w_ref[...]) + b_ref[...]


def _modulation(cond, w, b):
    m, d = cond.shape
    n = w.shape[1]
    tn = 512
    return pl.pallas_call(
        _mod_kernel,
        grid=(n // tn,),
        in_specs=[pl.BlockSpec((m, d), lambda j: (0, 0)),
                  pl.BlockSpec((d, tn), lambda j: (0, j)),
                  pl.BlockSpec((1, tn), lambda j: (0, j))],
        out_specs=pl.BlockSpec((m, tn), lambda j: (0, j)),
        out_shape=jax.ShapeDtypeStruct((m, n), f32),
        compiler_params=_cparams(("arbitrary",), 24),
        name="modulation",
    )(cond, w, b.reshape(1, n))


def _swap_halves(t):
    w = t.shape[-1]
    first = (_iota(t.shape, 1) % ATT_HEAD_DIM) < (ATT_HEAD_DIM // 2)
    return jnp.where(first, pltpu.roll(t, w - ATT_HEAD_DIM // 2, 1), pltpu.roll(t, ATT_HEAD_DIM // 2, 1))


ROPE_ALL, ROPE_FIRST, CONV = "rope_all", "rope_first", "conv"
CONV_CHUNK = 512


def _inproj_kernel(*refs, splits, rope, nt):
    it = iter(refs)
    xp_ref, x_ref, xn_ref, g_ref, mp_ref, w_ref, cw_ref = [next(it) for _ in range(7)]
    cos_ref, sin_ref = (next(it), next(it)) if rope else (None, None)
    outs = [next(it) for _ in splits]
    h_ref, ext_ref = next(it), next(it)
    t = pl.program_id(1)
    tm = x_ref.shape[1]
    hl = FFN_HALO
    mp = mp_ref[0]
    nm = lambda v: _norm_mod(v, g_ref[...], mp[0:1], mp[1:2])
    h_ref[0:hl, :] = jnp.where(t > 0, nm(xp_ref[0]), 0.0).astype(bf16)
    h_ref[hl:hl + tm, :] = nm(x_ref[0]).astype(bf16)
    h_ref[hl + tm:2 * hl + tm, :] = jnp.where(t < nt - 1, nm(xn_ref[0]), 0.0).astype(bf16)
    h = h_ref[hl:hl + tm, :]
    for (start, width, kind), o_ref in zip(splits, outs):
        if kind == CONV:
            continue
        r = jnp.dot(h, w_ref[:, start:start + width], preferred_element_type=f32)
        if kind in (ROPE_ALL, ROPE_FIRST):
            reps = width // LANE
            plain = reps - 1 if kind == ROPE_FIRST else 0
            one, zero = jnp.ones_like(cos_ref[...]), jnp.zeros_like(cos_ref[...])
            cs = jnp.concatenate([cos_ref[...]] * (reps - plain) + [one] * plain, axis=1)
            sn = jnp.concatenate([sin_ref[...]] * (reps - plain) + [zero] * plain, axis=1)
            r = r * cs + _swap_halves(r) * sn
        o_ref[0] = r

    (start, width, _), o_ref = [(s, o) for s, o in zip(splits, outs) if s[2] == CONV][0]
    taps = cw_ref.shape[0]
    pad = taps // 2
    nc = width // CONV_CHUNK

    def proj(c):
        cols = slice(c * CONV_CHUNK, (c + 1) * CONV_CHUNK)
        ext_ref[:, cols] = jnp.dot(h_ref[...], w_ref[:, start + c * CONV_CHUNK:start + (c + 1) * CONV_CHUNK],
                                   preferred_element_type=f32)

    def conv(c):
        cols = slice(c * CONV_CHUNK, (c + 1) * CONV_CHUNK)
        acc = cw_ref[0:1, cols] * ext_ref[pl.ds(hl - pad, tm), cols]
        for k in range(1, taps):
            acc = acc + cw_ref[k:k + 1, cols] * ext_ref[pl.ds(hl - pad + k, tm), cols]
        o_ref[0, :, cols] = _silu(acc)

    proj(0)
    for c in range(nc):
        if c + 1 < nc:
            proj(c + 1)
        conv(c)


def _inproj(x, g, mp, w, conv_w, splits, tm, rope_tabs=None):
    bsz, length, d = x.shape
    n = w.shape[1]
    nt = length // tm
    hl = FFN_HALO
    hb = tm // hl
    last_hb = length // hl - 1
    per_batch = mp.shape[0] != 1
    rope = rope_tabs is not None
    conv_width = [s[1] for s in splits if s[2] == CONV][0]
    resident = lambda shape: pl.BlockSpec(shape, lambda b, t: (0,) * len(shape), pipeline_mode=pl.Buffered(1))
    in_specs = [pl.BlockSpec((1, hl, d), lambda b, t: (b, jnp.maximum(t * hb - 1, 0), 0)),
                pl.BlockSpec((1, tm, d), lambda b, t: (b, t, 0)),
                pl.BlockSpec((1, hl, d), lambda b, t: (b, jnp.minimum((t + 1) * hb, last_hb), 0)),
                pl.BlockSpec((1, d), lambda b, t: (0, 0)),
                pl.BlockSpec((1, 6, d), (lambda b, t: (b, 0, 0)) if per_batch else (lambda b, t: (0, 0, 0))),
                resident((d, n)),
                resident(conv_w.shape)]
    args = [x, x, x, g.reshape(1, d), mp, w, conv_w]
    if rope:
        in_specs += [pl.BlockSpec((tm, LANE), lambda b, t: (t, 0))] * 2
        args += list(rope_tabs)
    out_w = sum(s[1] for s in splits)
    vmem = (2 * tm * d * 4 + d * n * 2 + 2 * tm * out_w * 4 + (tm + 2 * hl) * (conv_width * 4 + d * 2)) // (1 << 20) + 8
    return pl.pallas_call(
        functools.partial(_inproj_kernel, splits=tuple(splits), rope=rope, nt=nt),
        grid=(bsz, nt),
        in_specs=in_specs,
        out_specs=[pl.BlockSpec((1, tm, s[1]), lambda b, t: (b, t, 0)) for s in splits],
        out_shape=[jax.ShapeDtypeStruct((bsz, length, s[1]), f32) for s in splits],
        scratch_shapes=[pltpu.VMEM((tm + 2 * hl, d), bf16), pltpu.VMEM((tm + 2 * hl, conv_width), f32)],
        compiler_params=_cparams(("parallel", "arbitrary"), vmem),
        name="inproj",
    )(*args)


def _attn_kernel(*refs, has_local, nb):
    if has_local:
        sink_ref, q_ref, kp_ref, ko_ref, kn_ref, kc_ref, o_ref = refs
    else:
        sink_ref, q_ref, kc_ref, o_ref = refs
    i = pl.program_id(1)
    scale = ATT_HEAD_DIM ** -0.5
    q = q_ref[0]
    kvc = kc_ref[0]
    hd = ATT_HEAD_DIM
    blk = q.shape[0]
    kv_parts = [kvc]
    if has_local:
        kv_parts += [kp_ref[0], ko_ref[0], kn_ref[0]]
        n_ctx = kvc.shape[0]
        shape = (ATT_GROUP * blk, n_ctx + 3 * ATT_BLOCK)
        qi = _iota(shape, 0) % blk
        col = _iota(shape, 1)
        kp, kn = col - n_ctx, col - (n_ctx + 2 * ATT_BLOCK)
        bad_prev = (kp >= 0) & (kp < ATT_BLOCK) & ((kp < qi) | (i < 1))
        bad_next = (kn >= 0) & ((kn > qi) | (i > nb - 2))
        ok = jnp.logical_not(bad_prev | bad_next)
    scores, vals, sinks = [], [], []
    for kvh in range(ATT_KV_HEADS):
        ks, vs = slice(kvh * hd, (kvh + 1) * hd), slice(ATT_KV + kvh * hd, ATT_KV + (kvh + 1) * hd)
        heads = range(kvh * ATT_GROUP, (kvh + 1) * ATT_GROUP)
        qs = jnp.concatenate([q[:, h * hd:(h + 1) * hd] for h in heads], axis=0)
        keys = jnp.concatenate([part[:, ks] for part in kv_parts], axis=0)
        vals.append(jnp.concatenate([part[:, vs] for part in kv_parts], axis=0))
        sinks.append(jnp.concatenate([jnp.full((blk, 1), sink_ref[h], f32) for h in heads], axis=0))
        s = _bdot_nt(qs * scale, keys)
        scores.append(jnp.where(ok, s, NEG_INF) if has_local else s)
    for kvh in range(ATT_KV_HEADS):
        s, sink = scores[kvh], sinks[kvh]
        m = jnp.maximum(sink, jnp.max(s, axis=-1, keepdims=True))
        p = jnp.exp(s - m)
        denom = jnp.exp(sink - m) + jnp.sum(p, axis=-1, keepdims=True)
        o = _bdot(p, vals[kvh]) / denom
        for gi in range(ATT_GROUP):
            h = kvh * ATT_GROUP + gi
            o_ref[0, :, h * hd:(h + 1) * hd] = o[gi * blk:(gi + 1) * blk]


def _attention(q, kv, kv_ctx, sink, has_local):
    bsz, length, _ = q.shape
    ctx_len = kv_ctx.shape[1]
    nb = length // ATT_BLOCK
    kvw = 2 * ATT_KV
    in_specs = [pl.BlockSpec(memory_space=pltpu.SMEM),
                pl.BlockSpec((1, ATT_BLOCK, ATT_Q), lambda b, i: (b, i, 0))]
    args = [sink, q]
    if has_local:
        in_specs += [pl.BlockSpec((1, ATT_BLOCK, kvw), lambda b, i: (b, jnp.maximum(i - 1, 0), 0)),
                     pl.BlockSpec((1, ATT_BLOCK, kvw), lambda b, i: (b, i, 0)),
                     pl.BlockSpec((1, ATT_BLOCK, kvw), lambda b, i: (b, jnp.minimum(i + 1, nb - 1), 0))]
        args += [kv, kv, kv]
    in_specs.append(pl.BlockSpec((1, ctx_len, kvw), lambda b, i: (b, 0, 0)))
    args.append(kv_ctx)
    return pl.pallas_call(
        functools.partial(_attn_kernel, has_local=has_local, nb=nb),
        grid=(bsz, nb),
        in_specs=in_specs,
        out_specs=pl.BlockSpec((1, ATT_BLOCK, ATT_Q), lambda b, i: (b, i, 0)),
        out_shape=jax.ShapeDtypeStruct((bsz, length, ATT_Q), f32),
        compiler_params=_cparams(("parallel", "arbitrary"), 24),
        name="attention",
    )(*args)


def _blockdiag(x):
    left = _iota(x.shape, 1) < DN_HEAD_DIM
    zero = jnp.zeros_like(x)
    return jnp.concatenate([jnp.where(left, x, zero), jnp.where(left, zero, x)], axis=0)


def _dot3_bd(lhs, *xs):
    lh, ll = _split_bf16(lhs)
    his, los = zip(*[_split_bf16(x) for x in xs])
    bh = jnp.concatenate([_blockdiag(h) for h in his], axis=1)
    bl = jnp.concatenate([_blockdiag(l) for l in los], axis=1)
    d = functools.partial(jnp.dot, preferred_element_type=f32)
    return d(lh, bh) + (d(lh, bl) + d(ll, bh))


def _dn_step(qkvs, dbas, par_ref, s_ref, o_refs, with_out):
    c = DN_CHUNK
    hd = DN_HEAD_DIM
    row = _iota((c, LANE), 0)
    lane = _iota((c, LANE), 1)
    left = lane < hd
    colh = lane % hd
    incl = (row >= colh, row <= colh)
    strict = (row > colh, row < colh)
    last = (c - 1, 0)
    tri = [(_iota((c, c), 0) >= _iota((c, c), 1)).astype(f32), (_iota((c, c), 0) <= _iota((c, c), 1)).astype(f32)]
    same_head = (_iota((LANE, LANE), 0) < hd) == (_iota((LANE, LANE), 1) < hd)
    probs = [(d, j) for d in range(2) for j in range(DN_HEADS // 2)]

    sig, g_all, g_all_t = [], [], []
    for d in range(2):
        sig.append(jax.nn.sigmoid(dbas[d]))
        ld = -jnp.exp(par_ref[0:1, :]) * _softplus(dbas[d] + par_ref[1:2, :])
        g_all.append(_hdot(tri[d], ld))
        g_all_t.append(lax.dot_general(ld, tri[d], (((0,), (1,)), ((), ())), precision=HI,
                                       preferred_element_type=f32))

    def l2n(x):
        x2 = x * x
        ssa = jnp.sum(jnp.where(left, x2, 0.0), axis=-1, keepdims=True)
        ssb = jnp.sum(jnp.where(left, 0.0, x2), axis=-1, keepdims=True)
        return x * lax.rsqrt(jnp.where(left, ssa, ssb) + EPS)

    bt, eg, e_incl, kn, qn, rv, rk, kd, ge = {}, {}, {}, {}, {}, {}, {}, {}, {}
    for p in probs:
        d, j = p
        ha, hb = 2 * j, 2 * j + 1
        pair = lambda arr, base: jnp.where(left, arr[:, base + ha:base + ha + 1], arr[:, base + hb:base + hb + 1])
        g = pair(g_all[d], 2 * DN_HEADS + d * DN_HEADS)
        g_row = jnp.concatenate([g_all_t[d][2 * DN_HEADS + d * DN_HEADS + ha:2 * DN_HEADS + d * DN_HEADS + ha + 1, :],
                                 g_all_t[d][2 * DN_HEADS + d * DN_HEADS + hb:2 * DN_HEADS + d * DN_HEADS + hb + 1, :]],
                                axis=1)
        g_last = g[last[d]:last[d] + 1, :]
        kp = qkvs[d][:, DN_WIDTH + j * LANE:DN_WIDTH + (j + 1) * LANE]
        vp = qkvs[d][:, 2 * DN_WIDTH + j * LANE:2 * DN_WIDTH + (j + 1) * LANE]
        bt[p] = pair(sig[d], d * DN_HEADS)
        eg[p] = jnp.exp(g)
        e_incl[p] = jnp.exp(jnp.where(incl[d], g - g_row, -jnp.inf))
        kn[p] = l2n(kp)
        rv[p] = bt[p] * vp
        rk[p] = (bt[p] * eg[p]) * kn[p]
        kd[p] = kn[p] * jnp.exp(g_last - g)
        ge[p] = jnp.exp(g_last)
        if with_out:
            qn[p] = l2n(qkvs[d][:, j * LANE:(j + 1) * LANE]) * (hd ** -0.5)

    if with_out:
        qk = {p: _bdot_nt(jnp.concatenate([kn[p], qn[p]], axis=0), _blockdiag(kn[p].astype(bf16))) for p in probs}
    else:
        qk = {p: _bdot_nt(kn[p], _blockdiag(kn[p].astype(bf16))) for p in probs}
    a = {p: bt[p] * qk[p][:c] * jnp.where(strict[p[0]], e_incl[p], 0.0) for p in probs}

    nm = {p: -a[p] for p in probs}
    ak = {p: _dot3_bd(a[p], a[p]) for p in probs}
    for _ in range(4):
        r = {p: _dot3_bd(jnp.concatenate([nm[p], ak[p]], axis=0), ak[p]) for p in probs}
        nm = {p: nm[p] + ak[p] + r[p][:c] for p in probs}
        ak = {p: r[p][c:] for p in probs}
    r = {p: _dot3_bd(nm[p], ak[p]) for p in probs}
    nm = {p: nm[p] + ak[p] + r[p] for p in probs}

    uw = {p: _dot3_bd(nm[p], rv[p], rk[p]) for p in probs}
    s = {p: s_ref[0, p[0], p[1]] for p in probs}
    u = {p: rv[p] + uw[p][:, :LANE] - _bdot(rk[p] + uw[p][:, LANE:], s[p]) for p in probs}
    if with_out:
        for p in probs:
            d, j = p
            lhs = jnp.concatenate([qn[p] * eg[p], qk[p][c:] * e_incl[p]], axis=1)
            rhs = jnp.concatenate([s[p].astype(bf16), _blockdiag(u[p].astype(bf16))], axis=0)
            o_refs[d][0, :, j * LANE:(j + 1) * LANE] = _bdot(lhs, rhs)
    for p in probs:
        s_ref[0, p[0], p[1]] = ge[p] * s[p] + jnp.where(same_head, _bdot_tn(kd[p], u[p]), 0.0)


def _dn_kernel(*refs, with_out):
    if with_out:
        qf_ref, qb_ref, bf_ref, bb_ref, par_ref, s0_ref, of_ref, ob_ref, s_ref = refs
    else:
        qf_ref, qb_ref, bf_ref, bb_ref, par_ref, s0_ref, s_ref = refs
        of_ref = ob_ref = None

    @pl.when(pl.program_id(1) == 0)
    def _():
        s_ref[...] = s0_ref[...]

    _dn_step((qf_ref[0], qb_ref[0]), (bf_ref[0], bb_ref[0]), par_ref, s_ref, (of_ref, ob_ref), with_out)


def _dn_state_shape(bsz):
    return (bsz, 2, DN_HEADS // 2, 2 * DN_HEAD_DIM, 2 * DN_HEAD_DIM)


def _deltanet(qkv, dba, par, s0, with_out):
    bsz, length, qw = qkv.shape
    nc = length // DN_CHUNK
    c = DN_CHUNK
    st_shape = _dn_state_shape(bsz)
    st_spec = pl.BlockSpec((1,) + st_shape[1:], lambda b, s: (b, 0, 0, 0, 0))
    in_specs = [pl.BlockSpec((1, c, qw), lambda b, s: (b, s, 0)),
                pl.BlockSpec((1, c, qw), lambda b, s: (b, nc - 1 - s, 0)),
                pl.BlockSpec((1, c, LANE), lambda b, s: (b, s, 0)),
                pl.BlockSpec((1, c, LANE), lambda b, s: (b, nc - 1 - s, 0)),
                pl.BlockSpec((2, LANE), lambda b, s: (0, 0)),
                st_spec]
    out_specs, out_shape = [], []
    if with_out:
        out_specs += [pl.BlockSpec((1, c, DN_WIDTH), lambda b, s: (b, s, 0)),
                      pl.BlockSpec((1, c, DN_WIDTH), lambda b, s: (b, nc - 1 - s, 0))]
        out_shape += [jax.ShapeDtypeStruct((bsz, length, DN_WIDTH), f32)] * 2
    out_specs.append(st_spec)
    out_shape.append(jax.ShapeDtypeStruct(st_shape, f32))
    return pl.pallas_call(
        functools.partial(_dn_kernel, with_out=with_out),
        grid=(bsz, nc),
        in_specs=in_specs,
        out_specs=out_specs,
        out_shape=out_shape,
        compiler_params=_cparams(("parallel", "arbitrary"), 24),
        name="deltanet",
    )(qkv, qkv, dba, dba, par, s0)


def _hyb_out_kernel(att_ref, of_ref, ob_ref, z_ref, x_ref, mp_ref, g_ref, bd_ref, w_ref, o_ref):
    dn = of_ref[0] + ob_ref[0]
    sq_hi, sq_lo = _split_bf16(dn * dn)
    ms = (jnp.dot(sq_hi, bd_ref[...], preferred_element_type=f32)
          + jnp.dot(sq_lo, bd_ref[...], preferred_element_type=f32))
    gated = (dn * lax.rsqrt(ms + EPS) * g_ref[...]) * _silu(z_ref[0])
    m = _bdot(att_ref[0], w_ref[0:ATT_Q, :]) + _bdot(gated, w_ref[ATT_Q:ATT_Q + DN_WIDTH, :])
    o_ref[0] = x_ref[0] + mp_ref[0][2:3] * m


def _hyb_out(att, o_f, o_b, z, x, mp, dn_norm_g, w_out, tm):
    bsz, length, d = x.shape
    per_batch = mp.shape[0] != 1
    head_of = np.arange(DN_WIDTH) // DN_HEAD_DIM
    bd = jnp.asarray((head_of[:, None] == head_of[None, :]).astype(np.float32) / DN_HEAD_DIM, dtype=bf16)
    g_row = jnp.tile(dn_norm_g, DN_HEADS).reshape(1, DN_WIDTH)
    tok = lambda w: pl.BlockSpec((1, tm, w), lambda b, t: (b, t, 0))
    const = lambda shape: pl.BlockSpec(shape, lambda b, t: (0,) * len(shape))
    return pl.pallas_call(
        _hyb_out_kernel,
        grid=(bsz, length // tm),
        in_specs=[tok(ATT_Q), tok(DN_WIDTH), tok(DN_WIDTH), tok(DN_WIDTH), tok(d),
                  pl.BlockSpec((1, 6, d), (lambda b, t: (b, 0, 0)) if per_batch else (lambda b, t: (0, 0, 0))),
                  const((1, DN_WIDTH)), const((DN_WIDTH, DN_WIDTH)), const((ATT_Q + DN_WIDTH, d))],
        out_specs=tok(d),
        out_shape=jax.ShapeDtypeStruct((bsz, length, d), f32),
        compiler_params=_cparams(("parallel", "arbitrary"), 40),
        name="hyb_out",
    )(att, o_f, o_b, z, x, mp, g_row, bd, w_out)


def _ssd_out_kernel(yf_ref, yb_ref, xa_ref, xb_ref, z_ref, x_ref, mp_ref, dsk_ref, g_ref, w_ref, o_ref):
    xs = jnp.concatenate([xa_ref[0], xb_ref[0]], axis=1)
    y = (yf_ref[0] + yb_ref[0] + dsk_ref[...] * xs) * _silu(z_ref[0])
    gw = SSM_D_INNER // SSM_GROUPS
    m = None
    for gi in range(SSM_GROUPS):
        yg = y[:, gi * gw:(gi + 1) * gw]
        ms = jnp.mean(yg * yg, axis=-1, keepdims=True)
        part = _bdot(yg * lax.rsqrt(ms + EPS) * g_ref[:, gi * gw:(gi + 1) * gw], w_ref[gi * gw:(gi + 1) * gw, :])
        m = part if m is None else m + part
    o_ref[0] = x_ref[0] + mp_ref[0][2:3] * m


def _ssd_out(y_f, y_b, xbc_act, z, x, mp, d_skip, norm_g, w_out, tm):
    bsz, length, d = x.shape
    di = SSM_D_INNER
    half = di // 2
    tok = lambda w: pl.BlockSpec((1, tm, w), lambda b, t: (b, t, 0))
    const = lambda shape: pl.BlockSpec(shape, lambda b, t: (0,) * len(shape))
    return pl.pallas_call(
        _ssd_out_kernel,
        grid=(bsz, length // tm),
        in_specs=[tok(di), tok(di),
                  pl.BlockSpec((1, tm, half), lambda b, t: (b, t, 0)),
                  pl.BlockSpec((1, tm, half), lambda b, t: (b, t, 1)),
                  tok(di), tok(d),
                  pl.BlockSpec((1, 6, d), lambda b, t: (b, 0, 0)),
                  const((1, di)), const((1, di)), const((di, d))],
        out_specs=tok(d),
        out_shape=jax.ShapeDtypeStruct((bsz, length, d), f32),
        compiler_params=_cparams(("parallel", "arbitrary"), 48),
        name="ssd_out",
    )(y_f, y_b, xbc_act, xbc_act, z, x, mp,
      jnp.repeat(d_skip, SSM_HEAD_DIM).reshape(1, di), norm_g.reshape(1, di), w_out)


def _ffn_kernel(xp_ref, x_ref, xn_ref, g_ref, mp_ref, wu_ref, cw_ref, wd_ref, fg_ref,
                o_ref, h_ref, u_ref, a_ref, *, nt, nj, final_norm):
    t = pl.program_id(1)
    tm = x_ref.shape[1]
    fc = FFN_CHUNK
    hl = FFN_HALO
    pad = FFN_CONV // 2
    mp = mp_ref[0]
    nm = lambda v: _norm_mod(v, g_ref[...], mp[3:4], mp[4:5])
    h_ref[0:hl, :] = jnp.where(t > 0, nm(xp_ref[0]), 0.0).astype(bf16)
    h_ref[hl:hl + tm, :] = nm(x_ref[0]).astype(bf16)
    h_ref[hl + tm:2 * hl + tm, :] = jnp.where(t < nt - 1, nm(xn_ref[0]), 0.0).astype(bf16)

    def up(c):
        for base in (0, FFN_DIM):
            cols = slice(base + c * fc, base + (c + 1) * fc)
            u_ref[:, cols] = jnp.dot(h_ref[...], wu_ref[:, cols], preferred_element_type=f32)

    def conv_gate(c):
        vc, gc = c * fc, FFN_DIM + c * fc
        val = cw_ref[0:1, vc:vc + fc] * u_ref[pl.ds(hl - pad, tm), vc:vc + fc]
        gate = cw_ref[0:1, gc:gc + fc] * u_ref[pl.ds(hl - pad, tm), gc:gc + fc]
        for k in range(1, FFN_CONV):
            val = val + cw_ref[k:k + 1, vc:vc + fc] * u_ref[pl.ds(hl - pad + k, tm), vc:vc + fc]
            gate = gate + cw_ref[k:k + 1, gc:gc + fc] * u_ref[pl.ds(hl - pad + k, tm), gc:gc + fc]
        a_ref[:, c * fc:(c + 1) * fc] = (_silu(gate) * val).astype(bf16)

    up(0)
    for c in range(nj):
        if c + 1 < nj:
            up(c + 1)
        conv_gate(c)
    y = x_ref[0] + mp[5:6] * jnp.dot(a_ref[...], wd_ref[...], preferred_element_type=f32)
    if final_norm:
        y = y * lax.rsqrt(jnp.mean(y * y, axis=-1, keepdims=True) + EPS) * fg_ref[...]
    o_ref[0] = y


def _conv_ffn(x, g, mp, w_up, w_conv, w_down, tm, final_g=None):
    bsz, length, d = x.shape
    per_batch = mp.shape[0] != 1
    nt = length // tm
    hl = FFN_HALO
    hb = tm // hl
    last_hb = length // hl - 1
    nj = FFN_DIM // FFN_CHUNK
    fc = FFN_CHUNK
    final_norm = final_g is not None
    fg = (final_g if final_norm else g).reshape(1, d)
    ff = FFN_DIM
    resident = lambda shape: pl.BlockSpec(shape, lambda b, t: (0,) * len(shape), pipeline_mode=pl.Buffered(1))
    vmem = (4 * tm * d * 4 + (tm + 2 * hl) * (d * 2 + 2 * ff * 4) + tm * ff * 2 + 3 * d * ff * 2) // (1 << 20) + 8
    return pl.pallas_call(
        functools.partial(_ffn_kernel, nt=nt, nj=nj, final_norm=final_norm),
        grid=(bsz, nt),
        in_specs=[pl.BlockSpec((1, hl, d), lambda b, t: (b, jnp.maximum(t * hb - 1, 0), 0)),
                  pl.BlockSpec((1, tm, d), lambda b, t: (b, t, 0)),
                  pl.BlockSpec((1, hl, d), lambda b, t: (b, jnp.minimum((t + 1) * hb, last_hb), 0)),
                  pl.BlockSpec((1, d), lambda b, t: (0, 0)),
                  pl.BlockSpec((1, 6, d), (lambda b, t: (b, 0, 0)) if per_batch else (lambda b, t: (0, 0, 0))),
                  resident((d, 2 * ff)),
                  resident((FFN_CONV, 2 * ff)),
                  resident((ff, d)),
                  pl.BlockSpec((1, d), lambda b, t: (0, 0))],
        out_specs=pl.BlockSpec((1, tm, d), lambda b, t: (b, t, 0)),
        out_shape=jax.ShapeDtypeStruct((bsz, length, d), f32),
        scratch_shapes=[pltpu.VMEM((tm + 2 * hl, d), bf16),
                        pltpu.VMEM((tm + 2 * hl, 2 * ff), f32),
                        pltpu.VMEM((tm, ff), bf16)],
        compiler_params=_cparams(("parallel", "arbitrary"), vmem),
        name="conv_ffn",
    )(x, x, x, g.reshape(1, d), mp, w_up, w_conv, w_down, fg)


def _ssd_step(xbcs, dtrs, par_ref, s_ref, y_refs, with_y):
    c = SSM_CHUNK
    p = SSM_HEAD_DIM
    n = SSM_STATE
    row = _iota((c, c), 0)
    col = _iota((c, c), 1)
    incl = (row >= col, row <= col)
    last = (c - 1, 0)
    a_row = -jnp.exp(par_ref[0:1, :])
    g, g_t, dt_t, wdt_t, ge = [], [], [], [], []
    for d in range(2):
        dt = _softplus(dtrs[d] + par_ref[1:2, :])
        da = dt * a_row
        tri = incl[d].astype(f32)
        g.append(_hdot(tri, da))
        g_t.append(lax.dot_general(da, tri, (((0,), (1,)), ((), ())), precision=HI,
                                   preferred_element_type=f32))
        dt_t.append(dt.T)
        wdt_t.append(jnp.exp(g_t[d][:, last[d]:last[d] + 1] - g_t[d]) * dt_t[d])
        ge.append(jnp.exp(g[d][last[d]:last[d] + 1, :]))

    groups = [(d, gi) for d in range(2) for gi in range(SSM_GROUPS)]
    s_in, b_t, scores, y_inter = {}, {}, {}, {}
    for d, gi in groups:
        bg = xbcs[d][:, SSM_D_INNER + gi * n:SSM_D_INNER + (gi + 1) * n]
        s_in[d, gi] = s_ref[0, d, gi]
        b_t[d, gi] = bg.T
        if with_y:
            cg = xbcs[d][:, SSM_D_INNER + SSM_BC + gi * n:SSM_D_INNER + SSM_BC + (gi + 1) * n]
            scores[d, gi] = _bdot_nt(cg, bg)
            y_inter[d, gi] = _bdot(cg, s_in[d, gi])

    left = _iota((c, LANE), 1) < p
    left_row = _iota((1, LANE), 1) < p
    for d, gi in groups:
        for pj in range(SSM_HPG // 2):
            h0 = gi * SSM_HPG + 2 * pj
            cls = (d * SSM_HEADS + h0, d * SSM_HEADS + h0 + 1)
            ps = slice(pj * LANE, (pj + 1) * LANE)
            xs_bd = _blockdiag(xbcs[d][:, h0 * p:h0 * p + LANE].astype(bf16))
            bw = jnp.concatenate([b_t[d, gi] * wdt_t[d][cl:cl + 1, :] for cl in cls], axis=1)
            if with_y:
                gcs = [jnp.broadcast_to(g[d][:, cl:cl + 1], (c, c)) for cl in cls]
                m = jnp.concatenate(
                    [scores[d, gi] * jnp.exp(jnp.where(incl[d], gc - g_t[d][cl:cl + 1, :], -jnp.inf))
                     * dt_t[d][cl:cl + 1, :] for gc, cl in zip(gcs, cls)], axis=1)
                r = _bdot(jnp.concatenate([m, bw], axis=0), xs_bd)
                y_refs[d][0, :, h0 * p:h0 * p + LANE] = (
                    y_inter[d, gi][:, ps] * jnp.exp(jnp.where(left, gcs[0], gcs[1])) + r[:c])
                ds = r[c:]
            else:
                ds = _bdot(bw, xs_bd)
            ge_row = jnp.where(left_row, ge[d][:, cls[0]:cls[0] + 1], ge[d][:, cls[1]:cls[1] + 1])
            s_ref[0, d, gi, :, ps] = ge_row * s_in[d, gi][:, ps] + ds


def _ssd_kernel(*refs, with_y):
    if with_y:
        xf_ref, xb_ref, df_ref, db_ref, par_ref, s0_ref, yf_ref, yb_ref, s_ref = refs
    else:
        xf_ref, xb_ref, df_ref, db_ref, par_ref, s0_ref, s_ref = refs
        yf_ref = yb_ref = None

    @pl.when(pl.program_id(1) == 0)
    def _():
        s_ref[...] = s0_ref[...]

    _ssd_step((xf_ref[0], xb_ref[0]), (df_ref[0], db_ref[0]), par_ref, s_ref, (yf_ref, yb_ref), with_y)


def _ssd(xbc, dtr, par, s0, with_y):
    bsz, length, xw = xbc.shape
    c = SSM_CHUNK
    nc = length // c
    st_shape = (bsz, 2, SSM_GROUPS, SSM_STATE, SSM_GW)
    st_spec = pl.BlockSpec((1,) + st_shape[1:], lambda b, s: (b, 0, 0, 0, 0))
    in_specs = [pl.BlockSpec((1, c, xw), lambda b, s: (b, s, 0)),
                pl.BlockSpec((1, c, xw), lambda b, s: (b, nc - 1 - s, 0)),
                pl.BlockSpec((1, c, LANE), lambda b, s: (b, s, 0)),
                pl.BlockSpec((1, c, LANE), lambda b, s: (b, nc - 1 - s, 0)),
                pl.BlockSpec((2, LANE), lambda b, s: (0, 0)),
                st_spec]
    out_specs, out_shape = [], []
    if with_y:
        out_specs += [pl.BlockSpec((1, c, SSM_D_INNER), lambda b, s: (b, s, 0)),
                      pl.BlockSpec((1, c, SSM_D_INNER), lambda b, s: (b, nc - 1 - s, 0))]
        out_shape += [jax.ShapeDtypeStruct((bsz, length, SSM_D_INNER), f32)] * 2
    out_specs.append(st_spec)
    out_shape.append(jax.ShapeDtypeStruct(st_shape, f32))
    return pl.pallas_call(
        functools.partial(_ssd_kernel, with_y=with_y),
        grid=(bsz, nc),
        in_specs=in_specs,
        out_specs=out_specs,
        out_shape=out_shape,
        compiler_params=_cparams(("parallel", "arbitrary"), 40),
        name="ssd_scan",
    )(xbc, xbc, dtr, dtr, par, s0)


def _rope_tables(length):
    rows = length // GRID_W
    row = jnp.repeat(jnp.arange(rows, dtype=f32), GRID_W)
    col = jnp.tile(jnp.arange(GRID_W, dtype=f32), rows)
    n_freq = ATT_HEAD_DIM // 4
    inv = ROPE_BASE ** (-jnp.arange(n_freq, dtype=f32) / n_freq)
    ang = jnp.concatenate([row[:, None] * inv, col[:, None] * inv], axis=-1)
    cos, sin = jnp.cos(ang), jnp.sin(ang)
    reps = LANE // ATT_HEAD_DIM
    return (jnp.tile(jnp.concatenate([cos, cos], axis=-1), (1, reps)),
            jnp.tile(jnp.concatenate([-sin, sin], axis=-1), (1, reps)))


def _pad_cols(w, n):
    return jnp.pad(w, ((0, 0), (0, n - w.shape[1])))


def _lane_row(v, offset):
    return jnp.pad(v.reshape(1, -1), ((0, 0), (offset, LANE - offset - v.size)))


HYB_SPLITS = ((0, ATT_Q, ROPE_ALL), (ATT_Q, 2 * ATT_KV, ROPE_FIRST), (ATT_Q + 2 * ATT_KV, 3 * DN_WIDTH, CONV),
              (ATT_Q + 2 * ATT_KV + 3 * DN_WIDTH, DN_WIDTH, None),
              (ATT_Q + 2 * ATT_KV + 4 * DN_WIDTH, LANE, None))
SSM_SPLITS = ((0, SSM_D_INNER, None), (SSM_D_INNER, SSM_CONV_DIM, CONV),
              (SSM_D_INNER + SSM_CONV_DIM, LANE, None))


def _hybrid_layer(x, y_ctx, mp_lat, mp_ctx, norm_g, w_in, w_out, sink, dn_conv, dn_a_log, dn_dt_bias,
                  dn_norm_g, rope_tabs):
    bsz, length, _ = x.shape
    n_pad = HYB_SPLITS[-1][0] + LANE
    w_in_b = _pad_cols(w_in, n_pad).astype(bf16)
    w_out_b = w_out.astype(bf16)
    ctx_splits = tuple((s, w, k if k == CONV else None) for s, w, k in HYB_SPLITS)
    par = jnp.concatenate([_lane_row(dn_a_log, 2 * DN_HEADS), _lane_row(dn_dt_bias, 2 * DN_HEADS)], axis=0)
    s0 = jnp.zeros(_dn_state_shape(bsz), f32)

    q_c, kv_c, act_c, z_c, dba_c = _inproj(y_ctx, norm_g, mp_ctx, w_in_b, dn_conv, ctx_splits, 256)
    q_l, kv_l, act_l, z_l, dba_l = _inproj(x, norm_g, mp_lat, w_in_b, dn_conv, HYB_SPLITS, 512, rope_tabs)
    att_c = _attention(q_c, kv_c, kv_c, sink, False)
    att_l = _attention(q_l, kv_l, kv_c, sink, True)
    of_c, ob_c, s_c = _deltanet(act_c, dba_c, par, s0, True)
    of_l, ob_l, _ = _deltanet(act_l, dba_l, par, s_c, True)
    y_ctx = _hyb_out(att_c, of_c, ob_c, z_c, y_ctx, mp_ctx, dn_norm_g, w_out_b, 256)
    x = _hyb_out(att_l, of_l, ob_l, z_l, x, mp_lat, dn_norm_g, w_out_b, 512)
    return x, y_ctx


def _mamba_layer(x, y_ctx, mp_lat, mp_ctx, norm_g, w_in, conv_w, a_log, dt_bias, d_skip, ssm_norm_g, w_out):
    bsz = x.shape[0]
    n_pad = SSM_SPLITS[-1][0] + LANE
    w_in_b = _pad_cols(w_in, n_pad).astype(bf16)
    par = jnp.concatenate([_lane_row(a_log, 0), _lane_row(dt_bias, 0)], axis=0)
    s0 = jnp.zeros((bsz, 2, SSM_GROUPS, SSM_STATE, SSM_GW), f32)
    act_c, dt_c = _inproj(y_ctx, norm_g, mp_ctx, w_in_b, conv_w, SSM_SPLITS[1:], 256)
    z_l, act_l, dt_l = _inproj(x, norm_g, mp_lat, w_in_b, conv_w, SSM_SPLITS, 512)
    (s_c,) = _ssd(act_c, dt_c, par, s0, False)
    y_f, y_b, _ = _ssd(act_l, dt_l, par, s_c, True)
    return _ssd_out(y_f, y_b, act_l, z_l, x, mp_lat, d_skip, ssm_norm_g, w_out.astype(bf16), 256)


def kernel(x, c, ctx, c_ctx, mod_w, mod_b, norm_mix_g, norm_ffn_g, ffn_up, ffn_conv, ffn_down,
           hyb_w_in, hyb_w_out, att_sink, dn_conv, dn_a_log, dn_dt_bias, dn_norm_g,
           ssm_w_in, ssm_conv, ssm_a_log, ssm_dt_bias, ssm_d, ssm_norm_g, ssm_w_out, final_g):
    bsz, length, d = x.shape
    depth = mod_w.shape[0]
    rope_tabs = _rope_tables(length)
    cond = jnp.concatenate([c, c_ctx[None, :], jnp.zeros((2 * SUBLANE - bsz - 1, d), f32)], axis=0)
    y_ctx = ctx
    for i in range(depth):
        last = i == depth - 1
        mods = _modulation(cond, mod_w[i], mod_b[i]).reshape(2 * SUBLANE, 6, d)
        mp_lat, mp_ctx = mods[:bsz], mods[bsz:bsz + 1]
        if i % 2 == 0:
            e = i // 2
            x, y_ctx = _hybrid_layer(x, y_ctx, mp_lat, mp_ctx, norm_mix_g[i], hyb_w_in[e], hyb_w_out[e],
                                     att_sink[e], dn_conv[e], dn_a_log[e], dn_dt_bias[e], dn_norm_g[e], rope_tabs)
        else:
            assert last, "a Mamba layer that is not the last one would need the context outputs"
            j = i // 2
            x = _mamba_layer(x, y_ctx, mp_lat, mp_ctx, norm_mix_g[i], ssm_w_in[j], ssm_conv[j], ssm_a_log[j],
                             ssm_dt_bias[j], ssm_d[j], ssm_norm_g[j], ssm_w_out[j])
        w_up, w_down = ffn_up[i].astype(bf16), ffn_down[i].astype(bf16)
        x = _conv_ffn(x, norm_ffn_g[i], mp_lat, w_up, ffn_conv[i], w_down, 512, final_g if last else None)
        if not last:
            y_ctx = _conv_ffn(y_ctx, norm_ffn_g[i], mp_ctx, w_up, ffn_conv[i], w_down, 256)
    return x
```

```python
import functools
import math

import numpy as np
import jax
import jax.numpy as jnp
from jax import lax
from jax.experimental import pallas as pl
from jax.experimental.pallas import tpu as pltpu

f32 = jnp.float32
bf16 = jnp.bfloat16

D_MODEL = 1024
GRID_W = 64
EPS = 1e-6
NEG_INF = -1e30

ATT_HEADS = 8
ATT_KV_HEADS = 2
ATT_GROUP = ATT_HEADS // ATT_KV_HEADS
ATT_HEAD_DIM = 64
ATT_BLOCK = 128
ROPE_BASE = 10000.0
ATT_Q = ATT_HEADS * ATT_HEAD_DIM
ATT_KV = ATT_KV_HEADS * ATT_HEAD_DIM

DN_HEADS = 8
DN_HEAD_DIM = 64
DN_WIDTH = DN_HEADS * DN_HEAD_DIM
DN_CONV = 5
DN_CHUNK = 64
DN_BATCH = 4

SSM_D_INNER = 2 * D_MODEL
SSM_HEAD_DIM = 64
SSM_HEADS = SSM_D_INNER // SSM_HEAD_DIM
SSM_GROUPS = 4
SSM_HPG = SSM_HEADS // SSM_GROUPS
SSM_STATE = 128
SSM_CONV = 5
SSM_CHUNK = 128
SSM_BC = SSM_GROUPS * SSM_STATE
SSM_CONV_DIM = SSM_D_INNER + 2 * SSM_BC
SSM_GW = SSM_HPG * SSM_HEAD_DIM

FFN_DIM = 2816
FFN_CONV = 3
FFN_CHUNK = 256
FFN_ROW_BLOCKS = 2

LANE = 128
SUBLANE = 8
FFN_HALO = 2 * SUBLANE
VMEM_CAP_MB = 56

HI = lax.Precision.HIGHEST


def _cparams(sem, vmem_mb):
    return pltpu.CompilerParams(dimension_semantics=sem,
                                vmem_limit_bytes=min(vmem_mb, VMEM_CAP_MB) * 1024 * 1024)


def _bdot(a, b):
    return jnp.dot(a.astype(bf16), b.astype(bf16), preferred_element_type=f32)


def _bdot_nt(a, b):
    return lax.dot_general(a.astype(bf16), b.astype(bf16), (((1,), (1,)), ((), ())),
                           preferred_element_type=f32)


def _bdot_tn(a, b):
    return lax.dot_general(a.astype(bf16), b.astype(bf16), (((0,), (0,)), ((), ())),
                           preferred_element_type=f32)


def _hdot(a, b):
    return jnp.dot(a, b, precision=HI, preferred_element_type=f32)


def _hdot_nt(a, b):
    return lax.dot_general(a, b, (((1,), (1,)), ((), ())), precision=HI, preferred_element_type=f32)


def _split_bf16(a):
    hi = a.astype(bf16)
    lo = (a - hi.astype(f32)).astype(bf16)
    return hi, lo


def _dot3(a, b):
    ah, al = _split_bf16(a)
    bh, bl = _split_bf16(b)
    d = functools.partial(jnp.dot, preferred_element_type=f32)
    return d(ah, bh) + (d(ah, bl) + d(al, bh))


def _silu(x):
    return x * jax.nn.sigmoid(x)


def _softplus(x):
    return jnp.maximum(x, 0.0) + jnp.log1p(jnp.exp(-jnp.abs(x)))


def _norm_mod(x, g, sh, sc):
    ms = jnp.mean(x * x, axis=-1, keepdims=True)
    return (x * lax.rsqrt(ms + EPS) * g) * (1.0 + sc) + sh


def _iota(shape, dim):
    return lax.broadcasted_iota(jnp.int32, shape, dim)


def _mod_kernel(c_ref, w_ref, b_ref, o_ref):
    o_ref[...] = _bdot(_silu(c_ref[...]), w_ref[0]) + b_ref[0]


def _modulation(cond, w, b, layer):
    m, d = cond.shape
    depth, _, n = w.shape
    tn = 512
    return pl.pallas_call(
        _mod_kernel,
        grid=(n // tn,),
        in_specs=[pl.BlockSpec((m, d), lambda j: (0, 0)),
                  pl.BlockSpec((1, d, tn), lambda j: (layer, 0, j)),
                  pl.BlockSpec((1, 1, tn), lambda j: (layer, 0, j))],
        out_specs=pl.BlockSpec((m, tn), lambda j: (0, j)),
        out_shape=jax.ShapeDtypeStruct((m, n), f32),
        compiler_params=_cparams(("arbitrary",), 24),
        name="modulation",
    )(cond, w, b.reshape(depth, 1, n))


def _swap_halves(t):
    w = t.shape[-1]
    first = (_iota(t.shape, 1) % ATT_HEAD_DIM) < (ATT_HEAD_DIM // 2)
    return jnp.where(first, pltpu.roll(t, w - ATT_HEAD_DIM // 2, 1), pltpu.roll(t, ATT_HEAD_DIM // 2, 1))


ROPE_ALL, ROPE_FIRST, ATTN_PLAIN, CONV = "rope_all", "rope_first", "attn_plain", "conv"
CONV_CHUNK = 512


def _split_dtype(kind):
    return bf16 if kind in (ROPE_ALL, ROPE_FIRST, ATTN_PLAIN) else f32


def _inproj_kernel(*refs, splits, rope, nt):
    it = iter(refs)
    xp_ref, x_ref, xn_ref, g_ref, mp_ref, w_ref, cw_ref = [next(it) for _ in range(7)]
    cos_ref, sin_ref = (next(it), next(it)) if rope else (None, None)
    outs = [next(it) for _ in splits]
    h_ref, ext_ref = next(it), next(it)
    t = pl.program_id(1)
    tm = x_ref.shape[1]
    hl = FFN_HALO
    mp = mp_ref[0]
    nm = lambda v: _norm_mod(v, g_ref[...], mp[0:1], mp[1:2])
    h_ref[0:hl, :] = jnp.where(t > 0, nm(xp_ref[0]), 0.0).astype(bf16)
    h_ref[hl:hl + tm, :] = nm(x_ref[0]).astype(bf16)
    h_ref[hl + tm:2 * hl + tm, :] = jnp.where(t < nt - 1, nm(xn_ref[0]), 0.0).astype(bf16)
    h = h_ref[hl:hl + tm, :]
    for (start, width, kind), o_ref in zip(splits, outs):
        if kind == CONV:
            continue
        r = jnp.dot(h, w_ref[:, start:start + width], preferred_element_type=f32)
        if kind in (ROPE_ALL, ROPE_FIRST):
            reps = width // LANE
            plain = reps - 1 if kind == ROPE_FIRST else 0
            one, zero = jnp.ones_like(cos_ref[...]), jnp.zeros_like(cos_ref[...])
            cs = jnp.concatenate([cos_ref[...]] * (reps - plain) + [one] * plain, axis=1)
            sn = jnp.concatenate([sin_ref[...]] * (reps - plain) + [zero] * plain, axis=1)
            r = r * cs + _swap_halves(r) * sn
        o_ref[0] = r.astype(o_ref.dtype)

    (start, width, _), o_ref = [(s, o) for s, o in zip(splits, outs) if s[2] == CONV][0]
    taps = cw_ref.shape[0]
    pad = taps // 2
    nc = width // CONV_CHUNK

    def proj(c):
        cols = slice(c * CONV_CHUNK, (c + 1) * CONV_CHUNK)
        ext_ref[:, cols] = jnp.dot(h_ref[...], w_ref[:, start + c * CONV_CHUNK:start + (c + 1) * CONV_CHUNK],
                                   preferred_element_type=f32)

    def conv(c):
        cols = slice(c * CONV_CHUNK, (c + 1) * CONV_CHUNK)
        acc = cw_ref[0:1, cols] * ext_ref[pl.ds(hl - pad, tm), cols]
        for k in range(1, taps):
            acc = acc + cw_ref[k:k + 1, cols] * ext_ref[pl.ds(hl - pad + k, tm), cols]
        o_ref[0, :, cols] = _silu(acc)

    proj(0)
    for c in range(nc):
        if c + 1 < nc:
            proj(c + 1)
        conv(c)


def _inproj(x, g, mp, w, conv_w, splits, tm, rope_tabs=None):
    bsz, length, d = x.shape
    n = w.shape[1]
    nt = length // tm
    hl = FFN_HALO
    hb = tm // hl
    last_hb = length // hl - 1
    per_batch = mp.shape[0] != 1
    rope = rope_tabs is not None
    conv_width = [s[1] for s in splits if s[2] == CONV][0]
    resident = lambda shape: pl.BlockSpec(shape, lambda b, t: (0,) * len(shape), pipeline_mode=pl.Buffered(1))
    in_specs = [pl.BlockSpec((1, hl, d), lambda b, t: (b, jnp.maximum(t * hb - 1, 0), 0)),
                pl.BlockSpec((1, tm, d), lambda b, t: (b, t, 0)),
                pl.BlockSpec((1, hl, d), lambda b, t: (b, jnp.minimum((t + 1) * hb, last_hb), 0)),
                pl.BlockSpec((1, d), lambda b, t: (0, 0)),
                pl.BlockSpec((1, 6, d), (lambda b, t: (b, 0, 0)) if per_batch else (lambda b, t: (0, 0, 0))),
                resident((d, n)),
                resident(conv_w.shape)]
    args = [x, x, x, g.reshape(1, d), mp, w, conv_w]
    if rope:
        in_specs += [pl.BlockSpec((tm, LANE), lambda b, t: (t, 0))] * 2
        args += list(rope_tabs)
    out_w = sum(s[1] for s in splits)
    vmem = (2 * tm * d * 4 + d * n * 2 + 2 * tm * out_w * 4 + (tm + 2 * hl) * (conv_width * 4 + d * 2)) // (1 << 20) + 8
    return pl.pallas_call(
        functools.partial(_inproj_kernel, splits=tuple(splits), rope=rope, nt=nt),
        grid=(bsz, nt),
        in_specs=in_specs,
        out_specs=[pl.BlockSpec((1, tm, s[1]), lambda b, t: (b, t, 0)) for s in splits],
        out_shape=[jax.ShapeDtypeStruct((bsz, length, s[1]), _split_dtype(s[2])) for s in splits],
        scratch_shapes=[pltpu.VMEM((tm + 2 * hl, d), bf16), pltpu.VMEM((tm + 2 * hl, conv_width), f32)],
        compiler_params=_cparams(("parallel", "arbitrary"), vmem),
        name="inproj",
    )(*args)


def _attn_kernel(*refs, has_local, nb):
    if has_local:
        sink_ref, q_ref, kp_ref, ko_ref, kn_ref, kc_ref, o_ref = refs
    else:
        sink_ref, q_ref, kc_ref, o_ref = refs
    i = pl.program_id(1)
    scale = ATT_HEAD_DIM ** -0.5
    q = q_ref[0]
    kvc = kc_ref[0]
    hd = ATT_HEAD_DIM
    blk = q.shape[0]
    kv_parts = [kvc]
    if has_local:
        kv_parts += [kp_ref[0], ko_ref[0], kn_ref[0]]
        n_ctx = kvc.shape[0]
        qi = _iota((ATT_GROUP * blk, ATT_BLOCK), 0) % blk
        kj = _iota((ATT_GROUP * blk, ATT_BLOCK), 1)
        ok_prev = (kj >= qi) & (i >= 1)
        ok_next = (kj <= qi) & (i <= nb - 2)

        def mask(s):
            lo, hi = n_ctx + ATT_BLOCK, n_ctx + 2 * ATT_BLOCK
            return jnp.concatenate([s[:, :n_ctx], jnp.where(ok_prev, s[:, n_ctx:lo], NEG_INF), s[:, lo:hi],
                                    jnp.where(ok_next, s[:, hi:], NEG_INF)], axis=1)
    scores, vals, sinks = [], [], []
    for kvh in range(ATT_KV_HEADS):
        ks, vs = slice(kvh * hd, (kvh + 1) * hd), slice(ATT_KV + kvh * hd, ATT_KV + (kvh + 1) * hd)
        heads = range(kvh * ATT_GROUP, (kvh + 1) * ATT_GROUP)
        qs = jnp.concatenate([q[:, h * hd:(h + 1) * hd] for h in heads], axis=0)
        keys = jnp.concatenate([part[:, ks] for part in kv_parts], axis=0)
        vals.append(jnp.concatenate([part[:, vs] for part in kv_parts], axis=0))
        sinks.append(jnp.concatenate([jnp.full((blk, 1), sink_ref[h], f32) for h in heads], axis=0))
        s = _bdot_nt(qs * scale, keys)
        scores.append(mask(s) if has_local else s)
    for kvh in range(ATT_KV_HEADS):
        s, sink = scores[kvh], sinks[kvh]
        m = jnp.maximum(sink, jnp.max(s, axis=-1, keepdims=True))
        p = jnp.exp(s - m)
        denom = jnp.exp(sink - m) + jnp.sum(p, axis=-1, keepdims=True)
        o = _bdot(p, vals[kvh]) / denom
        for gi in range(ATT_GROUP):
            h = kvh * ATT_GROUP + gi
            o_ref[0, :, h * hd:(h + 1) * hd] = o[gi * blk:(gi + 1) * blk].astype(o_ref.dtype)


def _attention(q, kv, kv_ctx, sink, has_local):
    bsz, length, _ = q.shape
    ctx_len = kv_ctx.shape[1]
    nb = length // ATT_BLOCK
    kvw = 2 * ATT_KV
    in_specs = [pl.BlockSpec(memory_space=pltpu.SMEM),
                pl.BlockSpec((1, ATT_BLOCK, ATT_Q), lambda b, i: (b, i, 0))]
    args = [sink, q]
    if has_local:
        in_specs += [pl.BlockSpec((1, ATT_BLOCK, kvw), lambda b, i: (b, jnp.maximum(i - 1, 0), 0)),
                     pl.BlockSpec((1, ATT_BLOCK, kvw), lambda b, i: (b, i, 0)),
                     pl.BlockSpec((1, ATT_BLOCK, kvw), lambda b, i: (b, jnp.minimum(i + 1, nb - 1), 0))]
        args += [kv, kv, kv]
    in_specs.append(pl.BlockSpec((1, ctx_len, kvw), lambda b, i: (b, 0, 0)))
    args.append(kv_ctx)
    return pl.pallas_call(
        functools.partial(_attn_kernel, has_local=has_local, nb=nb),
        grid=(bsz, nb),
        in_specs=in_specs,
        out_specs=pl.BlockSpec((1, ATT_BLOCK, ATT_Q), lambda b, i: (b, i, 0)),
        out_shape=jax.ShapeDtypeStruct((bsz, length, ATT_Q), bf16),
        compiler_params=_cparams(("parallel", "arbitrary"), 24),
        name="attention",
    )(*args)


def _blockdiag(x):
    left = _iota(x.shape, 1) < DN_HEAD_DIM
    zero = jnp.zeros_like(x)
    return jnp.concatenate([jnp.where(left, x, zero), jnp.where(left, zero, x)], axis=0)


def _dot3_bd(lhs, *xs):
    lh, ll = _split_bf16(lhs)
    his, los = zip(*[_split_bf16(x) for x in xs])
    bh = jnp.concatenate([_blockdiag(h) for h in his], axis=1)
    bl = jnp.concatenate([_blockdiag(l) for l in los], axis=1)
    d = functools.partial(jnp.dot, preferred_element_type=f32)
    m = lhs.shape[0]
    both = d(jnp.concatenate([lh, ll], axis=0), bh)
    return both[:m] + (d(lh, bl) + both[m:])


def _dn_step(qkvs, dbas, par_ref, s_ref, o_refs, with_out):
    c = DN_CHUNK
    hd = DN_HEAD_DIM
    row = _iota((c, LANE), 0)
    lane = _iota((c, LANE), 1)
    left = lane < hd
    colh = lane % hd
    incl = (row >= colh, row <= colh)
    strict = (row > colh, row < colh)
    last = (c - 1, 0)
    tri = [(_iota((c, c), 0) >= _iota((c, c), 1)).astype(f32), (_iota((c, c), 0) <= _iota((c, c), 1)).astype(f32)]
    same_head = (_iota((LANE, LANE), 0) < hd) == (_iota((LANE, LANE), 1) < hd)
    nb = len(qkvs)
    probs = [(bb, d, j) for bb in range(nb) for d in range(2) for j in range(DN_HEADS // 2)]

    sig, g_all, g_all_t = {}, {}, {}
    for bb in range(nb):
        for d in range(2):
            sig[bb, d] = jax.nn.sigmoid(dbas[bb][d])
            ld = -jnp.exp(par_ref[0:1, :]) * _softplus(dbas[bb][d] + par_ref[1:2, :])
            g_all[bb, d] = _hdot(tri[d], ld)
            g_all_t[bb, d] = lax.dot_general(ld, tri[d], (((0,), (1,)), ((), ())), precision=HI,
                                             preferred_element_type=f32)

    def l2n(x):
        x2 = x * x
        ssa = jnp.sum(jnp.where(left, x2, 0.0), axis=-1, keepdims=True)
        ssb = jnp.sum(jnp.where(left, 0.0, x2), axis=-1, keepdims=True)
        return x * lax.rsqrt(jnp.where(left, ssa, ssb) + EPS)

    bt, eg, e_incl, kn, qn, rv, rk, kd, ge = {}, {}, {}, {}, {}, {}, {}, {}, {}
    for p in probs:
        bb, d, j = p
        ha, hb = 2 * j, 2 * j + 1
        pair = lambda arr, base: jnp.where(left, arr[:, base + ha:base + ha + 1], arr[:, base + hb:base + hb + 1])
        gcol = 2 * DN_HEADS + d * DN_HEADS
        g = pair(g_all[bb, d], gcol)
        g_row = jnp.concatenate([g_all_t[bb, d][gcol + ha:gcol + ha + 1, :],
                                 g_all_t[bb, d][gcol + hb:gcol + hb + 1, :]], axis=1)
        g_last = g[last[d]:last[d] + 1, :]
        kp = qkvs[bb][d][:, DN_WIDTH + j * LANE:DN_WIDTH + (j + 1) * LANE]
        vp = qkvs[bb][d][:, 2 * DN_WIDTH + j * LANE:2 * DN_WIDTH + (j + 1) * LANE]
        bt[p] = pair(sig[bb, d], d * DN_HEADS)
        eg[p] = jnp.exp(g)
        e_incl[p] = jnp.exp(jnp.where(incl[d], g - g_row, -jnp.inf))
        kn[p] = l2n(kp)
        rv[p] = bt[p] * vp
        rk[p] = (bt[p] * eg[p]) * kn[p]
        kd[p] = kn[p] * jnp.exp(g_last - g)
        ge[p] = jnp.exp(g_last)
        if with_out:
            qn[p] = l2n(qkvs[bb][d][:, j * LANE:(j + 1) * LANE]) * (hd ** -0.5)

    if with_out:
        qk = {p: _bdot_nt(jnp.concatenate([kn[p], qn[p]], axis=0), _blockdiag(kn[p].astype(bf16))) for p in probs}
    else:
        qk = {p: _bdot_nt(kn[p], _blockdiag(kn[p].astype(bf16))) for p in probs}
    a = {p: bt[p] * qk[p][:c] * jnp.where(strict[p[1]], e_incl[p], 0.0) for p in probs}

    nm = {p: -a[p] for p in probs}
    ak = {p: _dot3_bd(a[p], a[p]) for p in probs}
    for _ in range(4):
        r = {p: _dot3_bd(jnp.concatenate([nm[p], ak[p]], axis=0), ak[p]) for p in probs}
        nm = {p: nm[p] + ak[p] + r[p][:c] for p in probs}
        ak = {p: r[p][c:] for p in probs}
    r = {p: _dot3_bd(nm[p], ak[p]) for p in probs}
    nm = {p: nm[p] + ak[p] + r[p] for p in probs}

    uw = {p: _dot3_bd(nm[p], rv[p], rk[p]) for p in probs}
    s = {p: s_ref[p] for p in probs}
    u = {p: rv[p] + uw[p][:, :LANE] - _bdot(rk[p] + uw[p][:, LANE:], s[p]) for p in probs}
    if with_out:
        for p in probs:
            bb, d, j = p
            lhs = jnp.concatenate([qn[p] * eg[p], qk[p][c:] * e_incl[p]], axis=1)
            rhs = jnp.concatenate([s[p].astype(bf16), _blockdiag(u[p].astype(bf16))], axis=0)
            o_refs[d][bb, :, j * LANE:(j + 1) * LANE] = _bdot(lhs, rhs)
    for p in probs:
        s_ref[p] = ge[p] * s[p] + jnp.where(same_head, _bdot_tn(kd[p], u[p]), 0.0)


def _dn_kernel(*refs, with_out):
    if with_out:
        qf_ref, qb_ref, bf_ref, bb_ref, par_ref, s0_ref, of_ref, ob_ref, s_ref = refs
    else:
        qf_ref, qb_ref, bf_ref, bb_ref, par_ref, s0_ref, s_ref = refs
        of_ref = ob_ref = None

    @pl.when(pl.program_id(1) == 0)
    def _():
        s_ref[...] = s0_ref[...]

    nb = qf_ref.shape[0]
    _dn_step([(qf_ref[i], qb_ref[i]) for i in range(nb)], [(bf_ref[i], bb_ref[i]) for i in range(nb)],
             par_ref, s_ref, (of_ref, ob_ref), with_out)


def _dn_state_shape(bsz):
    return (bsz, 2, DN_HEADS // 2, 2 * DN_HEAD_DIM, 2 * DN_HEAD_DIM)


def _deltanet(qkv, dba, par, s0, with_out):
    bsz, length, qw = qkv.shape
    nc = length // DN_CHUNK
    c = DN_CHUNK
    nb = DN_BATCH
    st_shape = _dn_state_shape(bsz)
    st_spec = pl.BlockSpec((nb,) + st_shape[1:], lambda b, s: (b, 0, 0, 0, 0))
    in_specs = [pl.BlockSpec((nb, c, qw), lambda b, s: (b, s, 0)),
                pl.BlockSpec((nb, c, qw), lambda b, s: (b, nc - 1 - s, 0)),
                pl.BlockSpec((nb, c, LANE), lambda b, s: (b, s, 0)),
                pl.BlockSpec((nb, c, LANE), lambda b, s: (b, nc - 1 - s, 0)),
                pl.BlockSpec((2, LANE), lambda b, s: (0, 0)),
                st_spec]
    out_specs, out_shape = [], []
    if with_out:
        out_specs += [pl.BlockSpec((nb, c, DN_WIDTH), lambda b, s: (b, s, 0)),
                      pl.BlockSpec((nb, c, DN_WIDTH), lambda b, s: (b, nc - 1 - s, 0))]
        out_shape += [jax.ShapeDtypeStruct((bsz, length, DN_WIDTH), f32)] * 2
    out_specs.append(st_spec)
    out_shape.append(jax.ShapeDtypeStruct(st_shape, f32))
    return pl.pallas_call(
        functools.partial(_dn_kernel, with_out=with_out),
        grid=(bsz // nb, nc),
        in_specs=in_specs,
        out_specs=out_specs,
        out_shape=out_shape,
        compiler_params=_cparams(("parallel", "arbitrary"), 24),
        name="deltanet",
    )(qkv, qkv, dba, dba, par, s0)


def _hyb_out_kernel(att_ref, of_ref, ob_ref, z_ref, x_ref, mp_ref, g_ref, bd_ref, w_ref, o_ref):
    dn = of_ref[0] + ob_ref[0]
    sq_hi, sq_lo = _split_bf16(dn * dn)
    ms = (jnp.dot(sq_hi, bd_ref[...], preferred_element_type=f32)
          + jnp.dot(sq_lo, bd_ref[...], preferred_element_type=f32))
    gated = (dn * lax.rsqrt(ms + EPS) * g_ref[...]) * _silu(z_ref[0])
    m = _bdot(att_ref[0], w_ref[0:ATT_Q, :]) + _bdot(gated, w_ref[ATT_Q:ATT_Q + DN_WIDTH, :])
    o_ref[0] = x_ref[0] + mp_ref[0][2:3] * m


def _hyb_out(att, o_f, o_b, z, x, mp, dn_norm_g, w_out, tm):
    bsz, length, d = x.shape
    per_batch = mp.shape[0] != 1
    head_of = np.arange(DN_WIDTH) // DN_HEAD_DIM
    bd = jnp.asarray((head_of[:, None] == head_of[None, :]).astype(np.float32) / DN_HEAD_DIM, dtype=bf16)
    g_row = jnp.tile(dn_norm_g, DN_HEADS).reshape(1, DN_WIDTH)
    tok = lambda w: pl.BlockSpec((1, tm, w), lambda b, t: (b, t, 0))
    const = lambda shape: pl.BlockSpec(shape, lambda b, t: (0,) * len(shape))
    return pl.pallas_call(
        _hyb_out_kernel,
        grid=(bsz, length // tm),
        in_specs=[tok(ATT_Q), tok(DN_WIDTH), tok(DN_WIDTH), tok(DN_WIDTH), tok(d),
                  pl.BlockSpec((1, 6, d), (lambda b, t: (b, 0, 0)) if per_batch else (lambda b, t: (0, 0, 0))),
                  const((1, DN_WIDTH)), const((DN_WIDTH, DN_WIDTH)), const((ATT_Q + DN_WIDTH, d))],
        out_specs=tok(d),
        out_shape=jax.ShapeDtypeStruct((bsz, length, d), f32),
        compiler_params=_cparams(("parallel", "arbitrary"), 40),
        name="hyb_out",
    )(att, o_f, o_b, z, x, mp, g_row, bd, w_out)


def _ssd_out_kernel(yf_ref, yb_ref, z_ref, x_ref, mp_ref, g_ref, w_ref, o_ref):
    y = (yf_ref[0] + yb_ref[0]) * _silu(z_ref[0])
    gw = SSM_D_INNER // SSM_GROUPS
    m = None
    for gi in range(SSM_GROUPS):
        yg = y[:, gi * gw:(gi + 1) * gw]
        ms = jnp.mean(yg * yg, axis=-1, keepdims=True)
        part = _bdot(yg * lax.rsqrt(ms + EPS) * g_ref[:, gi * gw:(gi + 1) * gw], w_ref[gi * gw:(gi + 1) * gw, :])
        m = part if m is None else m + part
    o_ref[0] = x_ref[0] + mp_ref[0][2:3] * m


def _ssd_out(y_f, y_b, z, x, mp, norm_g, w_out, tm):
    bsz, length, d = x.shape
    di = SSM_D_INNER
    tok = lambda w: pl.BlockSpec((1, tm, w), lambda b, t: (b, t, 0))
    const = lambda shape: pl.BlockSpec(shape, lambda b, t: (0,) * len(shape))
    return pl.pallas_call(
        _ssd_out_kernel,
        grid=(bsz, length // tm),
        in_specs=[tok(di), tok(di), tok(di), tok(d),
                  pl.BlockSpec((1, 6, d), lambda b, t: (b, 0, 0)),
                  const((1, di)), const((di, d))],
        out_specs=tok(d),
        out_shape=jax.ShapeDtypeStruct((bsz, length, d), f32),
        compiler_params=_cparams(("parallel", "arbitrary"), 48),
        name="ssd_out",
    )(y_f, y_b, z, x, mp, norm_g.reshape(1, di), w_out)


def _ffn_kernel(xp_ref, x_ref, xn_ref, g_ref, mp_ref, wu_ref, cw_ref, wd_ref, fg_ref,
                o_ref, h_ref, u_ref, a_ref, *, nt, nj, final_norm):
    t = pl.program_id(1)
    tm = x_ref.shape[1]
    fc = FFN_CHUNK
    hl = FFN_HALO
    pad = FFN_CONV // 2
    mp = mp_ref[0]
    nm = lambda v: _norm_mod(v, g_ref[...], mp[3:4], mp[4:5])
    h_ref[0:hl, :] = jnp.where(t > 0, nm(xp_ref[0]), 0.0).astype(bf16)
    h_ref[hl:hl + tm, :] = nm(x_ref[0]).astype(bf16)
    h_ref[hl + tm:2 * hl + tm, :] = jnp.where(t < nt - 1, nm(xn_ref[0]), 0.0).astype(bf16)

    def up(c):
        for base in (0, FFN_DIM):
            cols = slice(base + c * fc, base + (c + 1) * fc)
            u_ref[:, cols] = jnp.dot(h_ref[...], wu_ref[:, cols], preferred_element_type=f32)

    def conv_gate(c):
        vc, gc = c * fc, FFN_DIM + c * fc
        val = cw_ref[0:1, vc:vc + fc] * u_ref[pl.ds(hl - pad, tm), vc:vc + fc]
        gate = cw_ref[0:1, gc:gc + fc] * u_ref[pl.ds(hl - pad, tm), gc:gc + fc]
        for k in range(1, FFN_CONV):
            val = val + cw_ref[k:k + 1, vc:vc + fc] * u_ref[pl.ds(hl - pad + k, tm), vc:vc + fc]
            gate = gate + cw_ref[k:k + 1, gc:gc + fc] * u_ref[pl.ds(hl - pad + k, tm), gc:gc + fc]
        a_ref[:, c * fc:(c + 1) * fc] = (_silu(gate) * val).astype(bf16)

    up(0)
    for c in range(nj):
        if c + 1 < nj:
            up(c + 1)
        conv_gate(c)
    y = x_ref[0] + mp[5:6] * jnp.dot(a_ref[...], wd_ref[...], preferred_element_type=f32)
    if final_norm:
        y = y * lax.rsqrt(jnp.mean(y * y, axis=-1, keepdims=True) + EPS) * fg_ref[...]
    o_ref[0] = y


def _conv_ffn(x, g, mp, w_up, w_conv, w_down, tm, final_g=None):
    bsz, length, d = x.shape
    per_batch = mp.shape[0] != 1
    nt = length // tm
    hl = FFN_HALO
    hb = tm // hl
    last_hb = length // hl - 1
    nj = FFN_DIM // FFN_CHUNK
    fc = FFN_CHUNK
    final_norm = final_g is not None
    fg = (final_g if final_norm else g).reshape(1, d)
    ff = FFN_DIM
    resident = lambda shape: pl.BlockSpec(shape, lambda b, t: (0,) * len(shape), pipeline_mode=pl.Buffered(1))
    vmem = (4 * tm * d * 4 + (tm + 2 * hl) * (d * 2 + 2 * ff * 4) + tm * ff * 2 + 3 * d * ff * 2) // (1 << 20) + 8
    return pl.pallas_call(
        functools.partial(_ffn_kernel, nt=nt, nj=nj, final_norm=final_norm),
        grid=(bsz, nt),
        in_specs=[pl.BlockSpec((1, hl, d), lambda b, t: (b, jnp.maximum(t * hb - 1, 0), 0)),
                  pl.BlockSpec((1, tm, d), lambda b, t: (b, t, 0)),
                  pl.BlockSpec((1, hl, d), lambda b, t: (b, jnp.minimum((t + 1) * hb, last_hb), 0)),
                  pl.BlockSpec((1, d), lambda b, t: (0, 0)),
                  pl.BlockSpec((1, 6, d), (lambda b, t: (b, 0, 0)) if per_batch else (lambda b, t: (0, 0, 0))),
                  resident((d, 2 * ff)),
                  resident((FFN_CONV, 2 * ff)),
                  resident((ff, d)),
                  pl.BlockSpec((1, d), lambda b, t: (0, 0))],
        out_specs=pl.BlockSpec((1, tm, d), lambda b, t: (b, t, 0)),
        out_shape=jax.ShapeDtypeStruct((bsz, length, d), f32),
        scratch_shapes=[pltpu.VMEM((tm + 2 * hl, d), bf16),
                        pltpu.VMEM((tm + 2 * hl, 2 * ff), f32),
                        pltpu.VMEM((tm, ff), bf16)],
        compiler_params=_cparams(("parallel", "arbitrary"), vmem),
        name="conv_ffn",
    )(x, x, x, g.reshape(1, d), mp, w_up, w_conv, w_down, fg)


def _ssd_step(xbcs, dtrs, par_ref, dsk_ref, s_ref, y_refs, with_y):
    c = SSM_CHUNK
    p = SSM_HEAD_DIM
    n = SSM_STATE
    row = _iota((c, c), 0)
    col = _iota((c, c), 1)
    incl = (row >= col, row <= col)
    last = (c - 1, 0)
    a_row = -jnp.exp(par_ref[0:1, :])
    g, g_t, dt_t, wdt_t, ge = [], [], [], [], []
    for d in range(2):
        dt = _softplus(dtrs[d] + par_ref[1:2, :])
        da = dt * a_row
        tri = incl[d].astype(f32)
        g.append(_hdot(tri, da))
        g_t.append(lax.dot_general(da, tri, (((0,), (1,)), ((), ())), precision=HI,
                                   preferred_element_type=f32))
        dt_t.append(dt.T)
        wdt_t.append(jnp.exp(g_t[d][:, last[d]:last[d] + 1] - g_t[d]) * dt_t[d])
        ge.append(jnp.exp(g[d][last[d]:last[d] + 1, :]))

    groups = [(d, gi) for d in range(2) for gi in range(SSM_GROUPS)]
    s_in, b_t, scores, y_inter = {}, {}, {}, {}
    for d, gi in groups:
        bg = xbcs[d][:, SSM_D_INNER + gi * n:SSM_D_INNER + (gi + 1) * n]
        s_in[d, gi] = s_ref[0, d, gi]
        b_t[d, gi] = bg.T
        if with_y:
            cg = xbcs[d][:, SSM_D_INNER + SSM_BC + gi * n:SSM_D_INNER + SSM_BC + (gi + 1) * n]
            scores[d, gi] = _bdot_nt(cg, bg)
            y_inter[d, gi] = _bdot(cg, s_in[d, gi])

    left = _iota((c, LANE), 1) < p
    left_row = _iota((1, LANE), 1) < p
    for d, gi in groups:
        for pj in range(SSM_HPG // 2):
            h0 = gi * SSM_HPG + 2 * pj
            cls = (d * SSM_HEADS + h0, d * SSM_HEADS + h0 + 1)
            ps = slice(pj * LANE, (pj + 1) * LANE)
            xs = xbcs[d][:, h0 * p:h0 * p + LANE]
            xs_bd = _blockdiag(xs.astype(bf16))
            bw = jnp.concatenate([b_t[d, gi] * wdt_t[d][cl:cl + 1, :] for cl in cls], axis=1)
            if with_y:
                gcs = [jnp.broadcast_to(g[d][:, cl:cl + 1], (c, c)) for cl in cls]
                m = jnp.concatenate(
                    [scores[d, gi] * jnp.exp(jnp.where(incl[d], gc - g_t[d][cl:cl + 1, :], -jnp.inf))
                     * dt_t[d][cl:cl + 1, :] for gc, cl in zip(gcs, cls)], axis=1)
                r = _bdot(jnp.concatenate([m, bw], axis=0), xs_bd)
                y = y_inter[d, gi][:, ps] * jnp.exp(jnp.where(left, gcs[0], gcs[1])) + r[:c]
                if d == 0:
                    y = y + dsk_ref[:, h0 * p:h0 * p + LANE] * xs
                y_refs[d][0, :, h0 * p:h0 * p + LANE] = y
                ds = r[c:]
            else:
                ds = _bdot(bw, xs_bd)
            ge_row = jnp.where(left_row, ge[d][:, cls[0]:cls[0] + 1], ge[d][:, cls[1]:cls[1] + 1])
            s_ref[0, d, gi, :, ps] = ge_row * s_in[d, gi][:, ps] + ds


def _ssd_kernel(*refs, with_y):
    if with_y:
        xf_ref, xb_ref, df_ref, db_ref, par_ref, dsk_ref, s0_ref, yf_ref, yb_ref, s_ref = refs
    else:
        xf_ref, xb_ref, df_ref, db_ref, par_ref, dsk_ref, s0_ref, s_ref = refs
        yf_ref = yb_ref = None

    @pl.when(pl.program_id(1) == 0)
    def _():
        s_ref[...] = s0_ref[...]

    _ssd_step((xf_ref[0], xb_ref[0]), (df_ref[0], db_ref[0]), par_ref, dsk_ref, s_ref, (yf_ref, yb_ref), with_y)


def _ssd(xbc, dtr, par, d_skip, s0, with_y):
    bsz, length, xw = xbc.shape
    c = SSM_CHUNK
    nc = length // c
    st_shape = (bsz, 2, SSM_GROUPS, SSM_STATE, SSM_GW)
    st_spec = pl.BlockSpec((1,) + st_shape[1:], lambda b, s: (b, 0, 0, 0, 0))
    in_specs = [pl.BlockSpec((1, c, xw), lambda b, s: (b, s, 0)),
                pl.BlockSpec((1, c, xw), lambda b, s: (b, nc - 1 - s, 0)),
                pl.BlockSpec((1, c, LANE), lambda b, s: (b, s, 0)),
                pl.BlockSpec((1, c, LANE), lambda b, s: (b, nc - 1 - s, 0)),
                pl.BlockSpec((2, LANE), lambda b, s: (0, 0)),
                pl.BlockSpec((1, SSM_D_INNER), lambda b, s: (0, 0)),
                st_spec]
    out_specs, out_shape = [], []
    if with_y:
        out_specs += [pl.BlockSpec((1, c, SSM_D_INNER), lambda b, s: (b, s, 0)),
                      pl.BlockSpec((1, c, SSM_D_INNER), lambda b, s: (b, nc - 1 - s, 0))]
        out_shape += [jax.ShapeDtypeStruct((bsz, length, SSM_D_INNER), f32)] * 2
    out_specs.append(st_spec)
    out_shape.append(jax.ShapeDtypeStruct(st_shape, f32))
    return pl.pallas_call(
        functools.partial(_ssd_kernel, with_y=with_y),
        grid=(bsz, nc),
        in_specs=in_specs,
        out_specs=out_specs,
        out_shape=out_shape,
        compiler_params=_cparams(("parallel", "arbitrary"), 40),
        name="ssd_scan",
    )(xbc, xbc, dtr, dtr, par, jnp.repeat(d_skip, SSM_HEAD_DIM).reshape(1, SSM_D_INNER), s0)


def _rope_tables(length):
    rows = length // GRID_W
    row = jnp.repeat(jnp.arange(rows, dtype=f32), GRID_W)
    col = jnp.tile(jnp.arange(GRID_W, dtype=f32), rows)
    n_freq = ATT_HEAD_DIM // 4
    inv = ROPE_BASE ** (-jnp.arange(n_freq, dtype=f32) / n_freq)
    ang = jnp.concatenate([row[:, None] * inv, col[:, None] * inv], axis=-1)
    cos, sin = jnp.cos(ang), jnp.sin(ang)
    reps = LANE // ATT_HEAD_DIM
    return (jnp.tile(jnp.concatenate([cos, cos], axis=-1), (1, reps)),
            jnp.tile(jnp.concatenate([-sin, sin], axis=-1), (1, reps)))


def _pad_cols(w, n):
    return jnp.pad(w, ((0, 0), (0, n - w.shape[1])))


def _lane_row(v, offset):
    return jnp.pad(v.reshape(1, -1), ((0, 0), (offset, LANE - offset - v.size)))


HYB_SPLITS = ((0, ATT_Q, ROPE_ALL), (ATT_Q, 2 * ATT_KV, ROPE_FIRST), (ATT_Q + 2 * ATT_KV, 3 * DN_WIDTH, CONV),
              (ATT_Q + 2 * ATT_KV + 3 * DN_WIDTH, DN_WIDTH, None),
              (ATT_Q + 2 * ATT_KV + 4 * DN_WIDTH, LANE, None))
SSM_SPLITS = ((0, SSM_D_INNER, None), (SSM_D_INNER, SSM_CONV_DIM, CONV),
              (SSM_D_INNER + SSM_CONV_DIM, LANE, None))


def _hybrid_layer(x, y_ctx, mp_lat, mp_ctx, norm_g, w_in, w_out, sink, dn_conv, dn_a_log, dn_dt_bias,
                  dn_norm_g, rope_tabs):
    bsz, length, _ = x.shape
    n_pad = HYB_SPLITS[-1][0] + LANE
    w_in_b = _pad_cols(w_in, n_pad).astype(bf16)
    w_out_b = w_out.astype(bf16)
    ctx_splits = tuple((s, w, ATTN_PLAIN if k in (ROPE_ALL, ROPE_FIRST) else k) for s, w, k in HYB_SPLITS)
    par = jnp.concatenate([_lane_row(dn_a_log, 2 * DN_HEADS), _lane_row(dn_dt_bias, 2 * DN_HEADS)], axis=0)
    s0 = jnp.zeros(_dn_state_shape(bsz), f32)

    q_c, kv_c, act_c, z_c, dba_c = _inproj(y_ctx, norm_g, mp_ctx, w_in_b, dn_conv, ctx_splits, 256)
    q_l, kv_l, act_l, z_l, dba_l = _inproj(x, norm_g, mp_lat, w_in_b, dn_conv, HYB_SPLITS, 512, rope_tabs)
    att_c = _attention(q_c, kv_c, kv_c, sink, False)
    att_l = _attention(q_l, kv_l, kv_c, sink, True)
    of_c, ob_c, s_c = _deltanet(act_c, dba_c, par, s0, True)
    of_l, ob_l, _ = _deltanet(act_l, dba_l, par, s_c, True)
    y_ctx = _hyb_out(att_c, of_c, ob_c, z_c, y_ctx, mp_ctx, dn_norm_g, w_out_b, 256)
    x = _hyb_out(att_l, of_l, ob_l, z_l, x, mp_lat, dn_norm_g, w_out_b, 512)
    return x, y_ctx


def _mamba_layer(x, y_ctx, mp_lat, mp_ctx, norm_g, w_in, conv_w, a_log, dt_bias, d_skip, ssm_norm_g, w_out):
    bsz = x.shape[0]
    n_pad = SSM_SPLITS[-1][0] + LANE
    w_in_b = _pad_cols(w_in, n_pad).astype(bf16)
    par = jnp.concatenate([_lane_row(a_log, 0), _lane_row(dt_bias, 0)], axis=0)
    s0 = jnp.zeros((bsz, 2, SSM_GROUPS, SSM_STATE, SSM_GW), f32)
    act_c, dt_c = _inproj(y_ctx, norm_g, mp_ctx, w_in_b, conv_w, SSM_SPLITS[1:], 256)
    z_l, act_l, dt_l = _inproj(x, norm_g, mp_lat, w_in_b, conv_w, SSM_SPLITS, 512)
    (s_c,) = _ssd(act_c, dt_c, par, d_skip, s0, False)
    y_f, y_b, _ = _ssd(act_l, dt_l, par, d_skip, s_c, True)
    return _ssd_out(y_f, y_b, z_l, x, mp_lat, ssm_norm_g, w_out.astype(bf16), 512)


def kernel(x, c, ctx, c_ctx, mod_w, mod_b, norm_mix_g, norm_ffn_g, ffn_up, ffn_conv, ffn_down,
           hyb_w_in, hyb_w_out, att_sink, dn_conv, dn_a_log, dn_dt_bias, dn_norm_g,
           ssm_w_in, ssm_conv, ssm_a_log, ssm_dt_bias, ssm_d, ssm_norm_g, ssm_w_out, final_g):
    bsz, length, d = x.shape
    depth = mod_w.shape[0]
    rope_tabs = _rope_tables(length)
    cond = jnp.concatenate([c, c_ctx[None, :], jnp.zeros((2 * SUBLANE - bsz - 1, d), f32)], axis=0)
    y_ctx = ctx
    for i in range(depth):
        last = i == depth - 1
        mods = _modulation(cond, mod_w, mod_b, i).reshape(2 * SUBLANE, 6, d)
        mp_lat, mp_ctx = mods[:bsz], mods[bsz:bsz + 1]
        if i % 2 == 0:
            e = i // 2
            x, y_ctx = _hybrid_layer(x, y_ctx, mp_lat, mp_ctx, norm_mix_g[i], hyb_w_in[e], hyb_w_out[e],
                                     att_sink[e], dn_conv[e], dn_a_log[e], dn_dt_bias[e], dn_norm_g[e], rope_tabs)
        else:
            assert last, "a Mamba layer that is not the last one would need the context outputs"
            j = i // 2
            x = _mamba_layer(x, y_ctx, mp_lat, mp_ctx, norm_mix_g[i], ssm_w_in[j], ssm_conv[j], ssm_a_log[j],
                             ssm_dt_bias[j], ssm_d[j], ssm_norm_g[j], ssm_w_out[j])
        w_up, w_down = ffn_up[i].astype(bf16), ffn_down[i].astype(bf16)
        x = _conv_ffn(x, norm_ffn_g[i], mp_lat, w_up, ffn_conv[i], w_down, 512, final_g if last else None)
        if not last:
            y_ctx = _conv_ffn(y_ctx, norm_ffn_g[i], mp_ctx, w_up, ffn_conv[i], w_down, 256)
    return x
```
